```python
import jax, jax.numpy as jnp
from jax import lax
import numpy as np

D_MODEL = 4096
BATCH = 2
SEQ = 4096
DEPTH = 2

GRID_W = 64
CTX_LEN = 256
HEAD_DIM = 128
W_MIX = D_MODEL
W_GROUP = W_MIX // 4
N_HEADS = W_GROUP // HEAD_DIM
KV_HEADS = N_HEADS // 4
Q_PER_KV = N_HEADS // KV_HEADS
CHUNK = 128
Q_BLOCK = 128
CONV_WIDTH = 31
CONV_PAD = CONV_WIDTH // 2
ROPE_THETA = 10000.0
ROPE_PAIRS = HEAD_DIM // 4
N_DIR = 2
EPS = 1e-6
PROJ_SIZES = (
    W_GROUP, W_GROUP, W_GROUP,
    W_GROUP, W_GROUP, W_GROUP,
    W_GROUP, KV_HEADS * HEAD_DIM, KV_HEADS * HEAD_DIM, W_GROUP,
    W_GROUP, W_GROUP, W_GROUP, W_GROUP, W_GROUP,
    N_DIR * 2 * N_HEADS,
)
P_IN = sum(PROJ_SIZES)
SPLIT_IDX = tuple(int(s) for s in np.cumsum(PROJ_SIZES)[:-1])

kernel_name = 'hymba_style_diffusion_hybrid'


def rmsnorm(x, g):
    xf = x.astype(jnp.float32)
    y = xf * lax.rsqrt(jnp.mean(xf * xf, axis=-1, keepdims=True) + EPS)
    return (y * g.astype(jnp.float32)).astype(x.dtype)


def layernorm(x, g, b):
    xf = x.astype(jnp.float32)
    mu = jnp.mean(xf, axis=-1, keepdims=True)
    xc = xf - mu
    y = xc * lax.rsqrt(jnp.mean(xc * xc, axis=-1, keepdims=True) + EPS)
    return (y * g.astype(jnp.float32) + b.astype(jnp.float32)).astype(x.dtype)


def heads(t, h):
    return t.reshape(t.shape[0], t.shape[1], h, HEAD_DIM)


def axial_rope(n):
    rows = n // GRID_W
    row = jnp.repeat(jnp.arange(rows), GRID_W)
    col = jnp.tile(jnp.arange(GRID_W), rows)
    freq = ROPE_THETA ** (-jnp.arange(ROPE_PAIRS, dtype=jnp.float32) / ROPE_PAIRS)
    ang = jnp.stack([row, col], axis=-1).astype(jnp.float32)[..., None] * freq
    return jnp.cos(ang), jnp.sin(ang)


def apply_rope(x, cos, sin):
    B_, n, H, d = x.shape
    xr = x.astype(jnp.float32).reshape(B_, n, H, 2, 2, ROPE_PAIRS)
    x1, x2 = xr[..., 0, :], xr[..., 1, :]
    c, s = cos[:, None], sin[:, None]
    out = jnp.stack([x1 * c - x2 * s, x2 * c + x1 * s], axis=-2)
    return out.reshape(B_, n, H, d).astype(x.dtype)


def chunk_gmlp(u, v, w_s, b_s, ln_g, ln_b):
    B_, T, W = v.shape
    vh = layernorm(v, ln_g, ln_b).reshape(B_, T // CHUNK, CHUNK, N_HEADS, HEAD_DIM)
    mixed = jnp.einsum('hpq,bnqhd->bnphd', w_s, vh) + b_s.T[None, None, :, :, None]
    return u * mixed.reshape(B_, T, W)


def conformer_conv(a, g, conv_w, conv_b, ln_g, ln_b):
    y = a * jax.nn.sigmoid(g)
    y = lax.conv_general_dilated(y, conv_w[:, None, :], (1,), [(CONV_PAD, CONV_PAD)],
                                 dimension_numbers=('NWC', 'WIO', 'NWC'),
                                 feature_group_count=y.shape[-1]) + conv_b
    return jax.nn.silu(layernorm(y, ln_g, ln_b))


def gqa_attend(q, k, v):
    B_, Lq = q.shape[:2]
    qg = q.reshape(B_, Lq, KV_HEADS, Q_PER_KV, HEAD_DIM)
    s = jnp.einsum('bqgrd,bkgd->bgrqk', qg, k).astype(jnp.float32) * HEAD_DIM ** -0.5
    p = jax.nn.softmax(s, axis=-1).astype(v.dtype)
    return jnp.einsum('bgrqk,bkgd->bqgrd', p, v).reshape(B_, Lq, W_GROUP)


def attend_latent(q, k_all, v_all):
    B_, N = q.shape[:2]
    nb = N // Q_BLOCK
    qb = jnp.moveaxis(q.reshape(B_, nb, Q_BLOCK, N_HEADS, HEAD_DIM), 1, 0)
    out = lax.map(lambda qq: gqa_attend(qq, k_all, v_all), qb)
    return jnp.moveaxis(out, 0, 1).reshape(B_, N, W_GROUP)


def mlstm_scan(q, k, v, ig, lf, state):
    B_, H, T, d = q.shape
    nc = T // CHUNK

    def to_chunks(t):
        return jnp.moveaxis(t.reshape(t.shape[:2] + (nc, CHUNK) + t.shape[3:]), 2, 0)

    causal = jnp.tril(jnp.ones((CHUNK, CHUNK), dtype=bool))

    def step(carry, inp):
        C, n, m = carry
        qc, kc, vc, ic, fc = inp
        b = jnp.cumsum(fc, axis=-1)
        dmat = jnp.where(causal, b[..., :, None] - b[..., None, :] + ic[..., None, :], -jnp.inf)
        inter = b + m[..., None]
        m_row = jnp.maximum(inter, jnp.max(dmat, axis=-1))
        w_intra = jnp.exp(dmat - m_row[..., None])
        w_inter = jnp.exp(inter - m_row)
        s = jnp.einsum('bhld,bhsd->bhls', qc, kc) * w_intra
        num = jnp.einsum('bhls,bhsd->bhld', s, vc) + w_inter[..., None] * jnp.einsum('bhvk,bhlk->bhlv', C, qc)
        den = jnp.sum(s, axis=-1) + w_inter * jnp.einsum('bhk,bhlk->bhl', n, qc)
        h = num / jnp.maximum(jnp.abs(den), jnp.exp(-m_row))[..., None]
        w_end, decay = w_intra[..., -1, :], w_inter[..., -1]
        C_new = decay[..., None, None] * C + jnp.einsum('bhs,bhsv,bhsk->bhvk', w_end, vc, kc)
        n_new = decay[..., None] * n + jnp.einsum('bhs,bhsk->bhk', w_end, kc)
        return (C_new, n_new, m_row[..., -1]), h

    state, h = lax.scan(step, state, (to_chunks(q), to_chunks(k), to_chunks(v), to_chunks(ig), to_chunks(lf)))
    return jnp.moveaxis(h, 0, 2).reshape(B_, H, T, d), state


def mlstm_inputs(qd, kd, vd, gates, i_bias, f_bias):
    B_, T, _ = qd.shape
    hd = lambda t: t.reshape(B_, T, N_HEADS, HEAD_DIM).transpose(0, 2, 1, 3).astype(jnp.float32)
    g = gates.reshape(B_, T, N_DIR, 2, N_HEADS).astype(jnp.float32)
    ig = jnp.transpose(g[..., 0, :] + i_bias.astype(jnp.float32), (2, 0, 3, 1))
    lf = jax.nn.log_sigmoid(jnp.transpose(g[..., 1, :] + f_bias.astype(jnp.float32), (2, 0, 3, 1)))
    return hd(qd), hd(kd) * HEAD_DIM ** -0.5, hd(vd), ig, lf


def bidir_mlstm(ctx_in, lat_in):
    qc, kc, vc, igc, lfc = ctx_in
    ql, kl, vl, igl, lfl = lat_in
    B_, H, _, d = qc.shape
    zero = (jnp.zeros((B_, H, d, d), jnp.float32), jnp.zeros((B_, H, d), jnp.float32),
            jnp.zeros((B_, H), jnp.float32))
    fl = lambda t: jnp.flip(t, axis=2)
    h_cf, st_f = mlstm_scan(qc, kc, vc, igc[0], lfc[0], zero)
    h_lf, _ = mlstm_scan(ql, kl, vl, igl[0], lfl[0], st_f)
    h_cb, st_b = mlstm_scan(fl(qc), fl(kc), fl(vc), fl(igc[1]), fl(lfc[1]), zero)
    h_lb, _ = mlstm_scan(fl(ql), fl(kl), fl(vl), fl(igl[1]), fl(lfl[1]), st_b)
    return h_cf + fl(h_cb), h_lf + fl(h_lb)


def mlstm_out(h, o, z, mh_norm_g):
    B_, H, T, d = h.shape
    hn = rmsnorm(h.transpose(0, 2, 1, 3), mh_norm_g.reshape(N_HEADS, HEAD_DIM)).reshape(B_, T, W_GROUP)
    return hn.astype(o.dtype) * jax.nn.sigmoid(o) * jax.nn.silu(z)


def hybrid_layer(x, ctx, c_act, cctx_act, w_ada, b_ada, norm_g, w_in, sgu_w, sgu_b, sgu_ln_g, sgu_ln_b,
                 conv_w, conv_b, conv_ln_g, conv_ln_b, q_norm_g, k_norm_g, i_bias, f_bias, mh_norm_g,
                 w_out, last):
    shift, scale, gate = jnp.split(c_act @ w_ada + b_ada, 3, axis=-1)
    shift_c, scale_c, gate_c = jnp.split(cctx_act @ w_ada + b_ada, 3, axis=-1)
    h_lat = rmsnorm(x, norm_g) * (1 + scale[:, None]) + shift[:, None]
    h_ctx = rmsnorm(ctx, norm_g) * (1 + scale_c) + shift_c
    pl = jnp.split(h_lat @ w_in, SPLIT_IDX, axis=-1)
    pc = jnp.split(h_ctx @ w_in, SPLIT_IDX, axis=-1)

    def local_branches(p):
        a = chunk_gmlp(p[0], p[1], sgu_w, sgu_b, sgu_ln_g, sgu_ln_b) * jax.nn.silu(p[2])
        b = conformer_conv(p[3], p[4], conv_w, conv_b, conv_ln_g, conv_ln_b) * jax.nn.silu(p[5])
        return a, b

    cos, sin = axial_rope(x.shape[1])
    q_l = apply_rope(rmsnorm(heads(pl[6], N_HEADS), q_norm_g), cos, sin)
    k_l = apply_rope(rmsnorm(heads(pl[7], KV_HEADS), k_norm_g), cos, sin)
    k_c = rmsnorm(heads(pc[7], KV_HEADS), k_norm_g)
    v_c = heads(pc[8], KV_HEADS)
    k_all = jnp.concatenate([k_c, k_l], axis=1)
    v_all = jnp.concatenate([v_c, heads(pl[8], KV_HEADS)], axis=1)
    att_l = attend_latent(q_l, k_all, v_all) * jax.nn.silu(pl[9])

    hd_c, hd_l = bidir_mlstm(mlstm_inputs(pc[10], pc[11], pc[12], pc[15], i_bias, f_bias),
                             mlstm_inputs(pl[10], pl[11], pl[12], pl[15], i_bias, f_bias))
    mem_l = mlstm_out(hd_l, pl[13], pl[14], mh_norm_g)

    a_l, b_l = local_branches(pl)
    x = x + gate[:, None] * (jnp.concatenate([a_l, b_l, att_l, mem_l], axis=-1) @ w_out)
    if not last:
        a_c, b_c = local_branches(pc)
        q_c = rmsnorm(heads(pc[6], N_HEADS), q_norm_g)
        att_c = gqa_attend(q_c, k_c, v_c) * jax.nn.silu(pc[9])
        mem_c = mlstm_out(hd_c, pc[13], pc[14], mh_norm_g)
        ctx = ctx + gate_c * (jnp.concatenate([a_c, b_c, att_c, mem_c], axis=-1) @ w_out)
    return x, ctx


def setup_inputs(seed: int = 0) -> dict:
    key = jax.random.key(seed)
    ks = jax.random.split(key, 24)
    nrm = lambda k, shape: jax.random.normal(k, shape, jnp.float32)
    f_base = jnp.linspace(3.0, 6.0, N_HEADS, dtype=jnp.float32)
    return {
        'x': nrm(ks[0], (BATCH, SEQ, D_MODEL)),
        'c': nrm(ks[1], (BATCH, D_MODEL)),
        'ctx': nrm(ks[2], (BATCH, CTX_LEN, D_MODEL)),
        'c_ctx': nrm(ks[3], (D_MODEL,)),
        'w_ada': nrm(ks[4], (DEPTH, D_MODEL, 3 * D_MODEL)) * (0.5 * D_MODEL ** -0.5),
        'b_ada': 0.02 * nrm(ks[5], (DEPTH, 3 * D_MODEL)),
        'norm_g': 1.0 + 0.02 * nrm(ks[6], (DEPTH, D_MODEL)),
        'w_in': nrm(ks[7], (DEPTH, D_MODEL, P_IN)) * D_MODEL ** -0.5,
        'sgu_w': nrm(ks[8], (DEPTH, N_HEADS, CHUNK, CHUNK)) * CHUNK ** -0.5,
        'sgu_b': 1.0 + 0.02 * nrm(ks[9], (DEPTH, N_HEADS, CHUNK)),
        'sgu_ln_g': 1.0 + 0.02 * nrm(ks[10], (DEPTH, W_GROUP)),
        'sgu_ln_b': 0.02 * nrm(ks[11], (DEPTH, W_GROUP)),
        'conv_w': nrm(ks[12], (DEPTH, CONV_WIDTH, W_GROUP)) * CONV_WIDTH ** -0.5,
        'conv_b': 0.02 * nrm(ks[13], (DEPTH, W_GROUP)),
        'conv_ln_g': 1.0 + 0.02 * nrm(ks[14], (DEPTH, W_GROUP)),
        'conv_ln_b': 0.02 * nrm(ks[15], (DEPTH, W_GROUP)),
        'q_norm_g': 1.0 + 0.02 * nrm(ks[16], (DEPTH, HEAD_DIM)),
        'k_norm_g': 1.0 + 0.02 * nrm(ks[17], (DEPTH, HEAD_DIM)),
        'mlstm_i_bias': 0.1 * nrm(ks[18], (DEPTH, N_DIR, N_HEADS)),
        'mlstm_f_bias': f_base + 0.1 * nrm(ks[19], (DEPTH, N_DIR, N_HEADS)),
        'mh_norm_g': 1.0 + 0.02 * nrm(ks[20], (DEPTH, W_GROUP)),
        'w_out': nrm(ks[21], (DEPTH, W_MIX, D_MODEL)) * W_MIX ** -0.5,
    }


def reference(x, c, ctx, c_ctx, w_ada, b_ada, norm_g, w_in, sgu_w, sgu_b, sgu_ln_g, sgu_ln_b, conv_w,
              conv_b, conv_ln_g, conv_ln_b, q_norm_g, k_norm_g, mlstm_i_bias, mlstm_f_bias, mh_norm_g, w_out):
    c_act = jax.nn.silu(c)
    cctx_act = jax.nn.silu(c_ctx)
    for l in range(DEPTH):
        x, ctx = hybrid_layer(x, ctx, c_act, cctx_act, w_ada[l], b_ada[l], norm_g[l], w_in[l], sgu_w[l],
                              sgu_b[l], sgu_ln_g[l], sgu_ln_b[l], conv_w[l], conv_b[l], conv_ln_g[l],
                              conv_ln_b[l], q_norm_g[l], k_norm_g[l], mlstm_i_bias[l], mlstm_f_bias[l],
                              mh_norm_g[l], w_out[l], l == DEPTH - 1)
    return x
```

```python
import functools

import jax
import jax.numpy as jnp
import numpy as np
from jax import lax
from jax.experimental import pallas as pl
from jax.experimental.pallas import tpu as pltpu

F32 = jnp.float32
BF16 = jnp.bfloat16

HEAD_DIM = 128
N_HEADS = 8
KV_HEADS = 2
Q_PER_KV = N_HEADS // KV_HEADS
W_GROUP = N_HEADS * HEAD_DIM
CHUNK = 128
CONV_WIDTH = 31
CONV_PAD = CONV_WIDTH // 2
GRID_W = 64
ROPE_THETA = 10000.0
ROPE_PAIRS = HEAD_DIM // 4
N_DIR = 2
EPS = 1e-6
GATE_COLS = N_DIR * 2 * N_HEADS
GATE_PAD = 128

COL_A_U, COL_A_V, COL_A_Z, COL_B_A, COL_B_G, COL_B_Z, COL_C_Q, COL_C_Z = range(8)
COL_D_Q, COL_D_K, COL_D_V, COL_D_O, COL_D_Z = range(8, 13)
COL_C_K = 13 * W_GROUP
COL_C_V = COL_C_K + KV_HEADS * HEAD_DIM
P_MAIN = COL_C_V + KV_HEADS * HEAD_DIM

ROW_TILE = 256
HALO = 16
V7X_VMEM_LIMIT = 56 * 1024 * 1024


def _silu(x):
    return x * jax.nn.sigmoid(x)


def _cparams(sem, vmem=None):
    return pltpu.CompilerParams(dimension_semantics=sem, vmem_limit_bytes=vmem)


def _ada_kernel(c_ref, w_ref, b_ref, o_ref):
    act = _silu(c_ref[...])
    o_ref[...] = jnp.dot(act.astype(BF16), w_ref[...].astype(BF16), preferred_element_type=F32) + b_ref[...]


def _ada_mod(cc, w_ada, b_ada, tn=512):
    L, D, N = w_ada.shape
    return pl.pallas_call(
        _ada_kernel,
        grid=(L, N // tn),
        in_specs=[pl.BlockSpec((8, D), lambda l, n: (0, 0)),
                  pl.BlockSpec((None, D, tn), lambda l, n: (l, 0, n)),
                  pl.BlockSpec((None, 1, tn), lambda l, n: (l, 0, n))],
        out_specs=pl.BlockSpec((None, 8, tn), lambda l, n: (l, 0, n)),
        out_shape=jax.ShapeDtypeStruct((L, 8, N), F32),
        compiler_params=_cparams(("parallel", "parallel")),
        name="ada_mod",
    )(cc, w_ada, b_ada.reshape(L, 1, N))


def _norm_kernel(x_ref, modb_ref, modc_ref, g_ref, wg_ref, wgt_ref, hn_ref, gates_ref, gatest_ref, *, D, tpb):
    is_ctx = (pl.program_id(0) % tpb) == 0
    x = x_ref[...]
    y = x * lax.rsqrt(jnp.mean(x * x, axis=-1, keepdims=True) + EPS) * g_ref[...]
    shift = jnp.where(is_ctx, modc_ref[:, 0:D], modb_ref[:, 0:D])
    scale = jnp.where(is_ctx, modc_ref[:, D:2 * D], modb_ref[:, D:2 * D])
    hb = (y * (1.0 + scale) + shift).astype(BF16)
    hn_ref[...] = hb
    gates_ref[...] = jnp.dot(hb, wg_ref[...], preferred_element_type=F32)
    gatest_ref[...] = lax.dot_general(wgt_ref[...], hb, (((1,), (1,)), ((), ())), preferred_element_type=F32)


def _norm_mod(h, mod, norm_g, wg, wgt, B, T):
    M, D = h.shape
    tpb = T // ROW_TILE
    return pl.pallas_call(
        functools.partial(_norm_kernel, D=D, tpb=tpb),
        grid=(M // ROW_TILE,),
        in_specs=[pl.BlockSpec((ROW_TILE, D), lambda i: (i, 0)),
                  pl.BlockSpec((None, 1, 3 * D), lambda i: (i // tpb, 0, 0)),
                  pl.BlockSpec((None, 1, 3 * D), lambda i: (B, 0, 0)),
                  pl.BlockSpec((1, D), lambda i: (0, 0)),
                  pl.BlockSpec((D, GATE_PAD), lambda i: (0, 0)),
                  pl.BlockSpec((GATE_PAD, D), lambda i: (0, 0))],
        out_specs=[pl.BlockSpec((ROW_TILE, D), lambda i: (i, 0)),
                   pl.BlockSpec((ROW_TILE, GATE_PAD), lambda i: (i, 0)),
                   pl.BlockSpec((GATE_PAD, ROW_TILE), lambda i: (0, i))],
        out_shape=[jax.ShapeDtypeStruct((M, D), BF16),
                   jax.ShapeDtypeStruct((M, GATE_PAD), F32),
                   jax.ShapeDtypeStruct((GATE_PAD, M), F32)],
        compiler_params=_cparams(("parallel",)),
        name="norm_mod",
    )(h, mod, mod, norm_g.reshape(1, D), wg, wgt)


def _mm_kernel(x_ref, w_ref, o_ref):
    o_ref[...] = jnp.dot(x_ref[...], w_ref[...], preferred_element_type=F32).astype(o_ref.dtype)


def _in_proj(hn, w, tm, tn):
    M, K = hn.shape
    N = w.shape[1]
    return pl.pallas_call(
        _mm_kernel,
        grid=(N // tn, M // tm),
        in_specs=[pl.BlockSpec((tm, K), lambda n, m: (m, 0)),
                  pl.BlockSpec((K, tn), lambda n, m: (0, n))],
        out_specs=pl.BlockSpec((tm, tn), lambda n, m: (m, n)),
        out_shape=jax.ShapeDtypeStruct((M, N), BF16),
        compiler_params=_cparams(("parallel", "parallel"), V7X_VMEM_LIMIT),
        name="in_proj",
    )(hn, w)


def _layernorm_rows(x, g, b):
    mu = jnp.mean(x, axis=-1, keepdims=True)
    xc = x - mu
    return xc * lax.rsqrt(jnp.mean(xc * xc, axis=-1, keepdims=True) + EPS) * g + b


def _gmlp_kernel(u_ref, v_ref, z_ref, ws_ref, bs_ref, lg_ref, lb_ref, o_ref, *, rows):
    for c in range(rows // CHUNK):
        rs = slice(c * CHUNK, (c + 1) * CHUNK)
        vn = _layernorm_rows(v_ref[rs, :].astype(F32), lg_ref[...], lb_ref[...]).astype(BF16)
        for hd in range(N_HEADS):
            cs = slice(hd * HEAD_DIM, (hd + 1) * HEAD_DIM)
            mixed = jnp.dot(ws_ref[hd], vn[:, cs], preferred_element_type=F32) + bs_ref[hd]
            u = u_ref[rs, cs].astype(F32)
            o_ref[rs, cs] = (u * mixed * _silu(z_ref[rs, cs].astype(F32))).astype(BF16)


def _gmlp(P, ws, bs, ln_g, ln_b, rows=512):
    M = P.shape[0]
    col = lambda j: pl.BlockSpec((rows, W_GROUP), lambda i: (i, j))
    return pl.pallas_call(
        functools.partial(_gmlp_kernel, rows=rows),
        grid=(M // rows,),
        in_specs=[col(COL_A_U), col(COL_A_V), col(COL_A_Z),
                  pl.BlockSpec((N_HEADS, CHUNK, CHUNK), lambda i: (0, 0, 0)),
                  pl.BlockSpec((N_HEADS, CHUNK, 1), lambda i: (0, 0, 0)),
                  pl.BlockSpec((1, W_GROUP), lambda i: (0, 0)),
                  pl.BlockSpec((1, W_GROUP), lambda i: (0, 0))],
        out_specs=pl.BlockSpec((rows, W_GROUP), lambda i: (i, 0)),
        out_shape=jax.ShapeDtypeStruct((M, W_GROUP), BF16),
        compiler_params=_cparams(("parallel",)),
        name="gmlp",
    )(P, P, P, ws, bs, ln_g.reshape(1, W_GROUP), ln_b.reshape(1, W_GROUP))


def _conv_kernel(a_ref, g_ref, z_ref, ap_ref, gp_ref, an_ref, gn_ref, w_ref, cb_ref, lg_ref, lb_ref, o_ref,
                 yext_ref, cbuf_ref, *, tpb):
    lt = pl.program_id(0) % tpb
    has_left = lt >= 2
    has_right = jnp.logical_and(lt >= 1, lt < tpb - 1)
    glu = lambda a, g: a.astype(F32) * jax.nn.sigmoid(g.astype(F32))
    yext_ref[HALO:HALO + ROW_TILE, :] = glu(a_ref[...], g_ref[...])
    yext_ref[0:HALO, :] = jnp.where(has_left, glu(ap_ref[...], gp_ref[...]), 0.0)
    yext_ref[HALO + ROW_TILE:, :] = jnp.where(has_right, glu(an_ref[...], gn_ref[...]), 0.0)
    off = HALO - CONV_PAD
    for c in range(W_GROUP // 128):
        cs = slice(c * 128, (c + 1) * 128)
        acc = jnp.zeros((ROW_TILE, 128), F32)
        for k in range(CONV_WIDTH):
            acc = acc + yext_ref[off + k:off + k + ROW_TILE, cs] * w_ref[k:k + 1, cs]
        cbuf_ref[:, cs] = acc + cb_ref[:, cs]
    y = _layernorm_rows(cbuf_ref[...], lg_ref[...], lb_ref[...])
    o_ref[...] = (_silu(y) * _silu(z_ref[...].astype(F32))).astype(BF16)


def _conv(P, conv_w, conv_b, ln_g, ln_b, T):
    M = P.shape[0]
    tpb = T // ROW_TILE
    hpt = ROW_TILE // HALO
    nhalo = M // HALO
    col = lambda j: pl.BlockSpec((ROW_TILE, W_GROUP), lambda i: (i, j))
    prev = lambda j: pl.BlockSpec((HALO, W_GROUP), lambda i: (jnp.maximum(i * hpt - 1, 0), j))
    nxt = lambda j: pl.BlockSpec((HALO, W_GROUP), lambda i: (jnp.minimum((i + 1) * hpt, nhalo - 1), j))
    vec = pl.BlockSpec((1, W_GROUP), lambda i: (0, 0))
    return pl.pallas_call(
        functools.partial(_conv_kernel, tpb=tpb),
        grid=(M // ROW_TILE,),
        in_specs=[col(COL_B_A), col(COL_B_G), col(COL_B_Z), prev(COL_B_A), prev(COL_B_G), nxt(COL_B_A),
                  nxt(COL_B_G), pl.BlockSpec((CONV_WIDTH, W_GROUP), lambda i: (0, 0)), vec, vec, vec],
        out_specs=pl.BlockSpec((ROW_TILE, W_GROUP), lambda i: (i, 0)),
        out_shape=jax.ShapeDtypeStruct((M, W_GROUP), BF16),
        scratch_shapes=[pltpu.VMEM((ROW_TILE + 2 * HALO, W_GROUP), F32), pltpu.VMEM((ROW_TILE, W_GROUP), F32)],
        compiler_params=_cparams(("parallel",)),
        name="conv",
    )(P, P, P, P, P, P, P, conv_w, conv_b.reshape(1, W_GROUP), ln_g.reshape(1, W_GROUP),
      ln_b.reshape(1, W_GROUP))


def _rope_tables(seq, ctx_len):
    t = jnp.arange(seq)
    pos = jnp.stack([t // GRID_W, t % GRID_W], axis=-1).astype(F32)
    freq = ROPE_THETA ** (-jnp.arange(ROPE_PAIRS, dtype=F32) / ROPE_PAIRS)
    ang = pos[:, :, None] * freq
    cos, sin = jnp.cos(ang), jnp.sin(ang)
    cfull = jnp.concatenate([cos[:, 0], cos[:, 0], cos[:, 1], cos[:, 1]], axis=-1)
    sfull = jnp.concatenate([-sin[:, 0], sin[:, 0], -sin[:, 1], sin[:, 1]], axis=-1)
    cfull = jnp.concatenate([jnp.ones((ctx_len, HEAD_DIM), F32), cfull], axis=0)
    sfull = jnp.concatenate([jnp.zeros((ctx_len, HEAD_DIM), F32), sfull], axis=0)
    return cfull, sfull


def _norm_rope(x, g, cos, sin, swap_lo):
    y = x * lax.rsqrt(jnp.mean(x * x, axis=-1, keepdims=True) + EPS) * g
    partner = jnp.where(swap_lo, pltpu.roll(y, HEAD_DIM - ROPE_PAIRS, axis=1), pltpu.roll(y, ROPE_PAIRS, axis=1))
    return y * cos + partner * sin


def _qk_kernel(q_ref, k_ref, cos_ref, sin_ref, qg_ref, kg_ref, qo_ref, ko_ref):
    cos, sin = cos_ref[...], sin_ref[...]
    lane = lax.broadcasted_iota(jnp.int32, (ROW_TILE, HEAD_DIM), 1)
    swap_lo = (lane & ROPE_PAIRS) == 0
    for hd in range(N_HEADS):
        cs = slice(hd * HEAD_DIM, (hd + 1) * HEAD_DIM)
        r = _norm_rope(q_ref[:, cs].astype(F32), qg_ref[...], cos, sin, swap_lo)
        qo_ref[:, cs] = (r * HEAD_DIM ** -0.5).astype(BF16)
    for hd in range(KV_HEADS):
        cs = slice(hd * HEAD_DIM, (hd + 1) * HEAD_DIM)
        ko_ref[:, cs] = _norm_rope(k_ref[:, cs].astype(F32), kg_ref[...], cos, sin, swap_lo).astype(BF16)


def _qk_prep(P, cos, sin, qg, kg, T):
    M = P.shape[0]
    tpb = T // ROW_TILE
    kvw = KV_HEADS * HEAD_DIM
    tab = pl.BlockSpec((ROW_TILE, HEAD_DIM), lambda i: (i % tpb, 0))
    vec = pl.BlockSpec((1, HEAD_DIM), lambda i: (0, 0))
    return pl.pallas_call(
        _qk_kernel,
        grid=(M // ROW_TILE,),
        in_specs=[pl.BlockSpec((ROW_TILE, W_GROUP), lambda i: (i, COL_C_Q)),
                  pl.BlockSpec((ROW_TILE, kvw), lambda i: (i, COL_C_K // kvw)), tab, tab, vec, vec],
        out_specs=[pl.BlockSpec((ROW_TILE, W_GROUP), lambda i: (i, 0)),
                   pl.BlockSpec((ROW_TILE, kvw), lambda i: (i, 0))],
        out_shape=[jax.ShapeDtypeStruct((M, W_GROUP), BF16), jax.ShapeDtypeStruct((M, kvw), BF16)],
        compiler_params=_cparams(("parallel",)),
        name="qk_prep",
    )(P, P, cos, sin, qg.reshape(1, HEAD_DIM), kg.reshape(1, HEAD_DIM))


def _attn_kernel(q_ref, k_ref, v_ref, z_ref, prev_ref, o_ref, *, tq):
    del prev_ref
    q = q_ref[...]
    qs = jnp.concatenate([q[:, j * HEAD_DIM:(j + 1) * HEAD_DIM] for j in range(Q_PER_KV)], axis=0)
    s = lax.dot_general(qs, k_ref[...], (((1,), (1,)), ((), ())), preferred_element_type=F32)
    p = jnp.exp(s - jnp.max(s, axis=-1, keepdims=True))
    denom = jnp.sum(p, axis=-1, keepdims=True)
    o = jnp.dot(p.astype(BF16), v_ref[...], preferred_element_type=F32) / denom
    for j in range(Q_PER_KV):
        cs = slice(j * HEAD_DIM, (j + 1) * HEAD_DIM)
        o_ref[:, cs] = (o[j * tq:(j + 1) * tq] * _silu(z_ref[:, cs].astype(F32))).astype(BF16)


def _attend(qr, kr, P, att, B, T, *, tq, q_off, n_q, n_k):
    M = qr.shape[0]
    gw = Q_PER_KV * HEAD_DIM
    qrow = lambda b, g, i: b * (T // tq) + q_off // tq + i
    krow = lambda b: b * (T // n_k)
    return pl.pallas_call(
        functools.partial(_attn_kernel, tq=tq),
        grid=(B, KV_HEADS, n_q // tq),
        in_specs=[pl.BlockSpec((tq, gw), lambda b, g, i: (qrow(b, g, i), g)),
                  pl.BlockSpec((n_k, HEAD_DIM), lambda b, g, i: (krow(b), g)),
                  pl.BlockSpec((n_k, HEAD_DIM), lambda b, g, i: (krow(b), COL_C_V // HEAD_DIM + g)),
                  pl.BlockSpec((tq, gw), lambda b, g, i: (qrow(b, g, i), COL_C_Z * W_GROUP // gw + g)),
                  pl.BlockSpec(memory_space=pl.ANY)],
        out_specs=pl.BlockSpec((tq, gw), lambda b, g, i: (qrow(b, g, i), g)),
        out_shape=jax.ShapeDtypeStruct((M, W_GROUP), BF16),
        input_output_aliases={4: 0},
        compiler_params=_cparams(("parallel", "parallel", "parallel"), V7X_VMEM_LIMIT),
        name=f"attn_q{n_q}_k{n_k}",
    )(qr, kr, P, P, att)


def _log_sigmoid(x):
    return jnp.minimum(x, 0.0) - jnp.log1p(jnp.exp(-jnp.abs(x)))


def _split3(x):
    hi = x.astype(BF16)
    r = x - hi.astype(F32)
    mid = r.astype(BF16)
    lo = (r - mid.astype(F32)).astype(BF16)
    return hi, mid, lo


def _mlstm_kernel(q_ref, k_ref, v_ref, g_ref, gt_ref, brow_ref, bcol_ref, o_ref, cext_ref, m_ref, *, direction):
    @pl.when(pl.program_id(1) == 0)
    def _():
        cext_ref[...] = jnp.zeros_like(cext_ref)
        m_ref[...] = jnp.zeros_like(m_ref)

    rev = direction == 1
    end = 0 if rev else CHUNK - 1
    scale = HEAD_DIM ** -0.5
    row = lax.broadcasted_iota(jnp.int32, (CHUNK, CHUNK), 0)
    colm = lax.broadcasted_iota(jnp.int32, (CHUNK, CHUNK), 1)
    allowed = (colm >= row) if rev else (colm <= row)
    tri = jnp.where(allowed, 1.0, 0.0).astype(BF16)

    pre = g_ref[...] + brow_ref[...]
    pre_t = gt_ref[...] + bcol_ref[...]
    nt = (((1,), (1,)), ((), ()))
    b = sum(jnp.dot(tri, part, preferred_element_type=F32) for part in _split3(_log_sigmoid(pre)))
    b_t = sum(lax.dot_general(part, tri, nt, preferred_element_type=F32) for part in _split3(_log_sigmoid(pre_t)))

    for hd in range(N_HEADS):
        cs = slice(hd * HEAD_DIM, (hd + 1) * HEAD_DIM)
        ic = direction * 2 * N_HEADS + hd
        fc = ic + N_HEADS
        b_col, i_col = b[:, fc:fc + 1], pre[:, ic:ic + 1]
        a_row = pre_t[ic:ic + 1, :] - b_t[fc:fc + 1, :]
        dm = jnp.where(allowed, b_col + a_row, -jnp.inf)
        m_prev = m_ref[hd]
        inter = b_col + m_prev
        m_row = jnp.maximum(inter, jnp.max(dm, axis=-1, keepdims=True))
        w_intra = jnp.exp(dm - m_row)
        w_inter = jnp.exp(inter - m_row) * scale
        qh, kh, vh = q_ref[:, cs], k_ref[:, cs], v_ref[:, cs]
        s = lax.dot_general(qh, kh, nt, preferred_element_type=F32) * (w_intra * scale)
        qc = lax.dot_general(qh, cext_ref[hd].astype(BF16), nt, preferred_element_type=F32)
        num = jnp.dot(s.astype(BF16), vh, preferred_element_type=F32) + w_inter * qc[:, 0:HEAD_DIM]
        den = jnp.sum(s, axis=-1, keepdims=True) + w_inter * qc[:, HEAD_DIM:HEAD_DIM + 1]
        o_ref[:, cs] = num / jnp.maximum(jnp.abs(den), jnp.exp(-m_row))

        m_end = m_row[end:end + 1, :]
        w_end = jnp.exp(b_col[end:end + 1, :] - b_col + i_col - m_end)
        decay = jnp.exp(inter[end:end + 1, :] - m_end)
        lhs = jnp.concatenate([(w_end * vh.astype(F32)).astype(BF16),
                               jnp.broadcast_to(w_end, (CHUNK, HEAD_DIM)).astype(BF16)], axis=1)
        upd = lax.dot_general(lhs, kh, (((0,), (0,)), ((), ())), preferred_element_type=F32)
        cext_ref[hd] = decay * cext_ref[hd] + upd
        m_ref[hd] = m_end


def _mlstm(P, gates, gates_t, bias_row, bias_col, B, T, ctx_len, direction):
    M = P.shape[0]
    nc = T // CHUNK
    ncc = ctx_len // CHUNK
    if direction == 0:
        chunk = lambda b, j: b * nc + j
    else:
        chunk = lambda b, j: b * nc + jnp.where(j < ncc, ncc - 1 - j, nc - 1 + ncc - j)
    col = lambda c: pl.BlockSpec((CHUNK, W_GROUP), lambda b, j: (chunk(b, j), c))
    return pl.pallas_call(
        functools.partial(_mlstm_kernel, direction=direction),
        grid=(B, nc),
        in_specs=[col(COL_D_Q), col(COL_D_K), col(COL_D_V),
                  pl.BlockSpec((CHUNK, GATE_PAD), lambda b, j: (chunk(b, j), 0)),
                  pl.BlockSpec((GATE_PAD, CHUNK), lambda b, j: (0, chunk(b, j))),
                  pl.BlockSpec((1, GATE_PAD), lambda b, j: (0, 0)),
                  pl.BlockSpec((GATE_PAD, 1), lambda b, j: (0, 0))],
        out_specs=pl.BlockSpec((CHUNK, W_GROUP), lambda b, j: (chunk(b, j), 0)),
        out_shape=jax.ShapeDtypeStruct((M, W_GROUP), F32),
        scratch_shapes=[pltpu.VMEM((N_HEADS, 2 * HEAD_DIM, HEAD_DIM), F32), pltpu.VMEM((N_HEADS, 1, 1), F32)],
        compiler_params=_cparams(("parallel", "arbitrary")),
        name=f"mlstm_dir{direction}",
    )(P, P, P, gates, gates_t, bias_row, bias_col)


def _memout_kernel(hf_ref, hb_ref, o_ref, z_ref, g_ref, out_ref):
    for hd in range(N_HEADS):
        cs = slice(hd * HEAD_DIM, (hd + 1) * HEAD_DIM)
        h = hf_ref[:, cs] + hb_ref[:, cs]
        hn = h * lax.rsqrt(jnp.mean(h * h, axis=-1, keepdims=True) + EPS) * g_ref[:, cs]
        out_ref[:, cs] = (hn * jax.nn.sigmoid(o_ref[:, cs].astype(F32)) * _silu(z_ref[:, cs].astype(F32))).astype(BF16)


def _mem_out(hf, hb, P, g, rows=512):
    M = P.shape[0]
    blk = lambda j: pl.BlockSpec((rows, W_GROUP), lambda i: (i, j))
    return pl.pallas_call(
        _memout_kernel,
        grid=(M // rows,),
        in_specs=[blk(0), blk(0), blk(COL_D_O), blk(COL_D_Z), pl.BlockSpec((1, W_GROUP), lambda i: (0, 0))],
        out_specs=blk(0),
        out_shape=jax.ShapeDtypeStruct((M, W_GROUP), BF16),
        compiler_params=_cparams(("parallel",)),
        name="mem_out",
    )(hf, hb, P, P, g.reshape(1, W_GROUP))


def _out_kernel(a_ref, b_ref, c_ref, d_ref, w_ref, res_ref, gb_ref, gc_ref, o_ref, *, tm, tiles_per_batch, ctx_len):
    x = jnp.concatenate([a_ref[...], b_ref[...], c_ref[...], d_ref[...]], axis=1)
    y = jnp.dot(x, w_ref[...], preferred_element_type=F32)
    local = (pl.program_id(1) % tiles_per_batch) * tm + lax.broadcasted_iota(jnp.int32, (tm, 1), 0)
    gate = jnp.where(local < ctx_len, gc_ref[...], gb_ref[...])
    o_ref[...] = res_ref[...] + gate * y


def _out_proj(branches, w, res, mod, B, T, ctx_len, tm, tn):
    M, D = res.shape
    tiles_per_batch = T // tm
    gcol = 2 * D // tn
    xin = pl.BlockSpec((tm, W_GROUP), lambda n, m: (m, 0))
    return pl.pallas_call(
        functools.partial(_out_kernel, tm=tm, tiles_per_batch=tiles_per_batch, ctx_len=ctx_len),
        grid=(D // tn, M // tm),
        in_specs=[xin, xin, xin, xin,
                  pl.BlockSpec((4 * W_GROUP, tn), lambda n, m: (0, n)),
                  pl.BlockSpec((tm, tn), lambda n, m: (m, n)),
                  pl.BlockSpec((None, 1, tn), lambda n, m: (m // tiles_per_batch, 0, gcol + n)),
                  pl.BlockSpec((None, 1, tn), lambda n, m: (B, 0, gcol + n))],
        out_specs=pl.BlockSpec((tm, tn), lambda n, m: (m, n)),
        out_shape=jax.ShapeDtypeStruct((M, D), F32),
        compiler_params=_cparams(("parallel", "parallel"), V7X_VMEM_LIMIT),
        name="out_proj",
    )(*branches, w, res, mod, mod)


def kernel(x, c, ctx, c_ctx, w_ada, b_ada, norm_g, w_in, sgu_w, sgu_b, sgu_ln_g, sgu_ln_b, conv_w, conv_b,
           conv_ln_g, conv_ln_b, q_norm_g, k_norm_g, mlstm_i_bias, mlstm_f_bias, mh_norm_g, w_out):
    B, seq, D = x.shape
    ctx_len = ctx.shape[1]
    depth = w_ada.shape[0]
    T = ctx_len + seq
    M = B * T
    assert ctx_len == ROW_TILE and T % ROW_TILE == 0 and M % 512 == 0 and D == 4 * W_GROUP and B < 8

    h = jnp.concatenate([ctx, x], axis=1).reshape(M, D)
    cc = jnp.zeros((8, D), F32).at[:B].set(c).at[B].set(c_ctx)
    mod_all = _ada_mod(cc, w_ada, b_ada)
    cos, sin = _rope_tables(seq, ctx_len)

    c_k, kv_w, n_main = 7 * W_GROUP, 2 * KV_HEADS * HEAD_DIM, P_MAIN
    assert w_in.shape[2] == n_main + GATE_COLS

    for l in range(depth):
        w_main = jnp.concatenate([w_in[l][:, :c_k], w_in[l][:, c_k + kv_w:n_main], w_in[l][:, c_k:c_k + kv_w]],
                                 axis=1).astype(BF16)
        wg = jnp.pad(w_in[l][:, n_main:n_main + GATE_COLS], ((0, 0), (0, GATE_PAD - GATE_COLS))).astype(BF16)
        mod = mod_all[l].reshape(8, 1, 3 * D)
        hn, gates, gates_t = _norm_mod(h, mod, norm_g[l], wg, wg.T, B, T)
        P = _in_proj(hn, w_main, tm=T // 4, tn=1152)

        a_out = _gmlp(P, sgu_w[l].astype(BF16), sgu_b[l].reshape(N_HEADS, CHUNK, 1), sgu_ln_g[l], sgu_ln_b[l])
        b_out = _conv(P, conv_w[l], conv_b[l], conv_ln_g[l], conv_ln_b[l], T)

        qr, kr = _qk_prep(P, cos, sin, q_norm_g[l], k_norm_g[l], T)
        att = jnp.zeros((M, W_GROUP), BF16)
        att = _attend(qr, kr, P, att, B, T, tq=CHUNK, q_off=ctx_len, n_q=seq, n_k=T)
        att = _attend(qr, kr, P, att, B, T, tq=ctx_len, q_off=0, n_q=ctx_len, n_k=ctx_len)

        bias = jnp.concatenate([mlstm_i_bias[l], mlstm_f_bias[l]], axis=-1).reshape(GATE_COLS)
        bias = jnp.pad(bias, (0, GATE_PAD - GATE_COLS))
        hf = _mlstm(P, gates, gates_t, bias.reshape(1, GATE_PAD), bias.reshape(GATE_PAD, 1), B, T, ctx_len, 0)
        hb = _mlstm(P, gates, gates_t, bias.reshape(1, GATE_PAD), bias.reshape(GATE_PAD, 1), B, T, ctx_len, 1)
        mem = _mem_out(hf, hb, P, mh_norm_g[l])

        h = _out_proj((a_out, b_out, att, mem), w_out[l].astype(BF16), h, mod, B, T, ctx_len, tm=T // 8, tn=1024)

    return h.reshape(B, T, D)[:, ctx_len:, :]
```

```python
import functools
import math

import jax
import jax.numpy as jnp
from jax import lax
from jax.experimental import pallas as pl
from jax.experimental.pallas import tpu as pltpu

F32 = jnp.float32
BF16 = jnp.bfloat16

HEAD_DIM = 128
N_HEADS = 8
KV_HEADS = 2
Q_PER_KV = N_HEADS // KV_HEADS
W_GROUP = N_HEADS * HEAD_DIM
HALF = W_GROUP // 2
CHUNK = 128
CONV_WIDTH = 31
CONV_PAD = CONV_WIDTH // 2
GRID_W = 64
ROPE_THETA = 10000.0
ROPE_PAIRS = HEAD_DIM // 4
N_DIR = 2
EPS = 1e-6
GATE_COLS = N_DIR * 2 * N_HEADS
GATE_PAD = 128

OFF_A_U, OFF_A_V, OFF_A_Z = 0, W_GROUP, 2 * W_GROUP
OFF_B_A, OFF_B_G, OFF_B_Z = 3 * W_GROUP, 4 * W_GROUP, 5 * W_GROUP
OFF_C_Q = 6 * W_GROUP
OFF_C_K = 7 * W_GROUP
OFF_C_V = OFF_C_K + KV_HEADS * HEAD_DIM
OFF_C_Z = OFF_C_V + KV_HEADS * HEAD_DIM
OFF_D_Q = OFF_C_Z + W_GROUP
OFF_D_K, OFF_D_V, OFF_D_O, OFF_D_Z = (OFF_D_Q + i * W_GROUP for i in range(1, 5))
P_MAIN = OFF_D_Z + W_GROUP

ROW_TILE = 256
HALO = 16
V7X_VMEM_LIMIT = 56 * 1024 * 1024
NT_DIMS = (((1,), (1,)), ((), ()))


def _silu(x):
    return x * jax.nn.sigmoid(x)


def _cparams(sem, vmem=None):
    return pltpu.CompilerParams(dimension_semantics=sem, vmem_limit_bytes=vmem)


def _ada_kernel(c_ref, w_ref, b_ref, o_ref):
    act = _silu(c_ref[...])
    o_ref[...] = jnp.dot(act.astype(BF16), w_ref[...].astype(BF16), preferred_element_type=F32) + b_ref[...]


def _ada_mod(cc, w_ada, b_ada, tn=512):
    L, D, N = w_ada.shape
    return pl.pallas_call(
        _ada_kernel,
        grid=(L, N // tn),
        in_specs=[pl.BlockSpec((8, D), lambda l, n: (0, 0)),
                  pl.BlockSpec((None, D, tn), lambda l, n: (l, 0, n)),
                  pl.BlockSpec((None, 1, tn), lambda l, n: (l, 0, n))],
        out_specs=pl.BlockSpec((None, 8, tn), lambda l, n: (l, 0, n)),
        out_shape=jax.ShapeDtypeStruct((L, 8, N), F32),
        compiler_params=_cparams(("parallel", "parallel")),
        name="ada_mod",
    )(cc, w_ada, b_ada.reshape(L, 1, N))


def _norm_kernel(x_ref, c_ref, mod_ref, g_ref, wg_ref, wgt_ref, hn_ref, gates_ref, gatest_ref, *, D, n_lat):
    x = jnp.where(pl.program_id(0) >= n_lat, c_ref[...], x_ref[...])
    y = x * lax.rsqrt(jnp.mean(x * x, axis=-1, keepdims=True) + EPS) * g_ref[...]
    hb = (y * (1.0 + mod_ref[:, D:2 * D]) + mod_ref[:, 0:D]).astype(BF16)
    hn_ref[...] = hb
    gates_ref[...] = jnp.dot(hb, wg_ref[...], preferred_element_type=F32)
    gatest_ref[...] = lax.dot_general(wgt_ref[...], hb, NT_DIMS, preferred_element_type=F32)


def _norm_mod(h_lat, h_ctx, mod, norm_g, wg, wgt, B, seq):
    D = h_lat.shape[1]
    n_lat = h_lat.shape[0] // ROW_TILE
    n_tiles = n_lat + h_ctx.shape[0] // ROW_TILE
    M = n_tiles * ROW_TILE
    tps = seq // ROW_TILE
    return pl.pallas_call(
        functools.partial(_norm_kernel, D=D, n_lat=n_lat),
        grid=(n_tiles,),
        in_specs=[pl.BlockSpec((ROW_TILE, D), lambda i: (jnp.minimum(i, n_lat - 1), 0)),
                  pl.BlockSpec((ROW_TILE, D), lambda i: (jnp.maximum(i - n_lat, 0), 0)),
                  pl.BlockSpec((None, 1, 3 * D), lambda i: (jnp.where(i < n_lat, i // tps, B), 0, 0)),
                  pl.BlockSpec((1, D), lambda i: (0, 0)),
                  pl.BlockSpec((D, GATE_PAD), lambda i: (0, 0)),
                  pl.BlockSpec((GATE_PAD, D), lambda i: (0, 0))],
        out_specs=[pl.BlockSpec((ROW_TILE, D), lambda i: (i, 0)),
                   pl.BlockSpec((ROW_TILE, GATE_PAD), lambda i: (i, 0)),
                   pl.BlockSpec((GATE_PAD, ROW_TILE), lambda i: (0, i))],
        out_shape=[jax.ShapeDtypeStruct((M, D), BF16),
                   jax.ShapeDtypeStruct((M, GATE_PAD), F32),
                   jax.ShapeDtypeStruct((GATE_PAD, M), F32)],
        compiler_params=_cparams(("parallel",)),
        name="norm_mod",
    )(h_lat, h_ctx, mod, norm_g.reshape(1, D), wg, wgt)


def _mm_kernel(x_ref, w_ref, o_ref):
    o_ref[...] = jnp.dot(x_ref[...], w_ref[...], preferred_element_type=F32).astype(o_ref.dtype)


def _in_proj(hn, w_all, layer, tm, tn):
    M, K = hn.shape
    return pl.pallas_call(
        _mm_kernel,
        grid=(P_MAIN // tn, M // tm),
        in_specs=[pl.BlockSpec((tm, K), lambda n, m: (m, 0)),
                  pl.BlockSpec((None, K, tn), lambda n, m: (layer, 0, n))],
        out_specs=pl.BlockSpec((tm, tn), lambda n, m: (m, n)),
        out_shape=jax.ShapeDtypeStruct((M, P_MAIN), BF16),
        compiler_params=_cparams(("parallel", "parallel"), V7X_VMEM_LIMIT),
        name="in_proj",
    )(hn, w_all)


def _layernorm_rows(x, g, b):
    mu = jnp.mean(x, axis=-1, keepdims=True)
    xc = x - mu
    return xc * lax.rsqrt(jnp.mean(xc * xc, axis=-1, keepdims=True) + EPS) * g + b


def _gmlp_kernel(u_ref, v_ref, z_ref, ws_ref, bs_ref, lg_ref, lb_ref, o_ref, *, rows):
    for c in range(rows // CHUNK):
        rs = slice(c * CHUNK, (c + 1) * CHUNK)
        vn = _layernorm_rows(v_ref[rs, :].astype(F32), lg_ref[...], lb_ref[...]).astype(BF16)
        for hd in range(N_HEADS):
            cs = slice(hd * HEAD_DIM, (hd + 1) * HEAD_DIM)
            mixed = jnp.dot(ws_ref[hd], vn[:, cs], preferred_element_type=F32) + bs_ref[hd]
            u = u_ref[rs, cs].astype(F32)
            o_ref[rs, cs] = (u * mixed * _silu(z_ref[rs, cs].astype(F32))).astype(BF16)


def _gmlp(P, n_rows, ws, bs, ln_g, ln_b, rows=512):
    col = lambda off: pl.BlockSpec((rows, W_GROUP), lambda i: (i, off // W_GROUP))
    return pl.pallas_call(
        functools.partial(_gmlp_kernel, rows=rows),
        grid=(n_rows // rows,),
        in_specs=[col(OFF_A_U), col(OFF_A_V), col(OFF_A_Z),
                  pl.BlockSpec((N_HEADS, CHUNK, CHUNK), lambda i: (0, 0, 0)),
                  pl.BlockSpec((N_HEADS, CHUNK, 1), lambda i: (0, 0, 0)),
                  pl.BlockSpec((1, W_GROUP), lambda i: (0, 0)),
                  pl.BlockSpec((1, W_GROUP), lambda i: (0, 0))],
        out_specs=pl.BlockSpec((rows, W_GROUP), lambda i: (i, 0)),
        out_shape=jax.ShapeDtypeStruct((P.shape[0], W_GROUP), BF16),
        compiler_params=_cparams(("parallel",)),
        name="gmlp",
    )(P, P, P, ws, bs, ln_g.reshape(1, W_GROUP), ln_b.reshape(1, W_GROUP))


def _conv_kernel(a_ref, g_ref, z_ref, ap_ref, gp_ref, an_ref, gn_ref, w_ref, cb_ref, lg_ref, lb_ref, o_ref,
                 yext_ref, cbuf_ref, *, tps, n_lat):
    i = pl.program_id(0)
    is_lat = i < n_lat
    has_left = jnp.logical_and(is_lat, i % tps != 0)
    has_right = jnp.logical_and(is_lat, i % tps != tps - 1)
    glu = lambda a, g: a.astype(F32) * jax.nn.sigmoid(g.astype(F32))
    yext_ref[HALO:HALO + ROW_TILE, :] = glu(a_ref[...], g_ref[...])
    yext_ref[0:HALO, :] = jnp.where(has_left, glu(ap_ref[...], gp_ref[...]), 0.0)
    yext_ref[HALO + ROW_TILE:, :] = jnp.where(has_right, glu(an_ref[...], gn_ref[...]), 0.0)
    off = HALO - CONV_PAD
    for c in range(W_GROUP // 128):
        cs = slice(c * 128, (c + 1) * 128)
        acc = jnp.zeros((ROW_TILE, 128), F32)
        for k in range(CONV_WIDTH):
            acc = acc + yext_ref[off + k:off + k + ROW_TILE, cs] * w_ref[k:k + 1, cs]
        cbuf_ref[:, cs] = acc + cb_ref[:, cs]
    y = _layernorm_rows(cbuf_ref[...], lg_ref[...], lb_ref[...])
    o_ref[...] = (_silu(y) * _silu(z_ref[...].astype(F32))).astype(BF16)


def _conv(P, n_rows, n_lat_rows, seq, conv_w, conv_b, ln_g, ln_b):
    hpt = ROW_TILE // HALO
    nhalo = P.shape[0] // HALO
    col = lambda off: pl.BlockSpec((ROW_TILE, W_GROUP), lambda i: (i, off // W_GROUP))
    prev = lambda off: pl.BlockSpec((HALO, W_GROUP), lambda i: (jnp.maximum(i * hpt - 1, 0), off // W_GROUP))
    nxt = lambda off: pl.BlockSpec((HALO, W_GROUP),
                                   lambda i: (jnp.minimum((i + 1) * hpt, nhalo - 1), off // W_GROUP))
    vec = pl.BlockSpec((1, W_GROUP), lambda i: (0, 0))
    return pl.pallas_call(
        functools.partial(_conv_kernel, tps=seq // ROW_TILE, n_lat=n_lat_rows // ROW_TILE),
        grid=(n_rows // ROW_TILE,),
        in_specs=[col(OFF_B_A), col(OFF_B_G), col(OFF_B_Z), prev(OFF_B_A), prev(OFF_B_G), nxt(OFF_B_A),
                  nxt(OFF_B_G), pl.BlockSpec((CONV_WIDTH, W_GROUP), lambda i: (0, 0)), vec, vec, vec],
        out_specs=pl.BlockSpec((ROW_TILE, W_GROUP), lambda i: (i, 0)),
        out_shape=jax.ShapeDtypeStruct((P.shape[0], W_GROUP), BF16),
        scratch_shapes=[pltpu.VMEM((ROW_TILE + 2 * HALO, W_GROUP), F32), pltpu.VMEM((ROW_TILE, W_GROUP), F32)],
        compiler_params=_cparams(("parallel",)),
        name="conv",
    )(P, P, P, P, P, P, P, conv_w, conv_b.reshape(1, W_GROUP), ln_g.reshape(1, W_GROUP),
      ln_b.reshape(1, W_GROUP))


def _rope_tables(seq):
    t = jnp.arange(seq)
    pos = jnp.stack([t // GRID_W, t % GRID_W], axis=-1).astype(F32)
    freq = ROPE_THETA ** (-jnp.arange(ROPE_PAIRS, dtype=F32) / ROPE_PAIRS)
    ang = pos[:, :, None] * freq
    cos, sin = jnp.cos(ang), jnp.sin(ang)
    cfull = jnp.concatenate([cos[:, 0], cos[:, 0], cos[:, 1], cos[:, 1]], axis=-1)
    sfull = jnp.concatenate([-sin[:, 0], sin[:, 0], -sin[:, 1], sin[:, 1]], axis=-1)
    cfull = jnp.concatenate([cfull, jnp.ones((ROW_TILE, HEAD_DIM), F32)], axis=0)
    sfull = jnp.concatenate([sfull, jnp.zeros((ROW_TILE, HEAD_DIM), F32)], axis=0)
    return cfull, sfull


def _norm_rope(x, g, cos, sin, swap_lo):
    y = x * lax.rsqrt(jnp.mean(x * x, axis=-1, keepdims=True) + EPS) * g
    partner = jnp.where(swap_lo, pltpu.roll(y, HEAD_DIM - ROPE_PAIRS, axis=1), pltpu.roll(y, ROPE_PAIRS, axis=1))
    return y * cos + partner * sin


def _qk_kernel(q_ref, k_ref, cos_ref, sin_ref, qg_ref, kg_ref, qo_ref, ko_ref):
    cos, sin = cos_ref[...], sin_ref[...]
    lane = lax.broadcasted_iota(jnp.int32, (ROW_TILE, HEAD_DIM), 1)
    swap_lo = (lane & ROPE_PAIRS) == 0
    q_scale = HEAD_DIM ** -0.5 * math.log2(math.e)
    for hd in range(N_HEADS):
        cs = slice(hd * HEAD_DIM, (hd + 1) * HEAD_DIM)
        r = _norm_rope(q_ref[:, cs].astype(F32), qg_ref[...], cos, sin, swap_lo)
        qo_ref[:, cs] = (r * q_scale).astype(BF16)
    for hd in range(KV_HEADS):
        cs = slice(hd * HEAD_DIM, (hd + 1) * HEAD_DIM)
        ko_ref[:, cs] = _norm_rope(k_ref[:, cs].astype(F32), kg_ref[...], cos, sin, swap_lo).astype(BF16)


def _qk_prep(P, cos, sin, qg, kg, n_lat_rows, seq):
    M = P.shape[0]
    tps = seq // ROW_TILE
    n_lat = n_lat_rows // ROW_TILE
    kvw = KV_HEADS * HEAD_DIM
    tab = pl.BlockSpec((ROW_TILE, HEAD_DIM), lambda i: (jnp.where(i < n_lat, i % tps, tps), 0))
    vec = pl.BlockSpec((1, HEAD_DIM), lambda i: (0, 0))
    return pl.pallas_call(
        _qk_kernel,
        grid=(M // ROW_TILE,),
        in_specs=[pl.BlockSpec((ROW_TILE, W_GROUP), lambda i: (i, OFF_C_Q // W_GROUP)),
                  pl.BlockSpec((ROW_TILE, kvw), lambda i: (i, OFF_C_K // kvw)), tab, tab, vec, vec],
        out_specs=[pl.BlockSpec((ROW_TILE, W_GROUP), lambda i: (i, 0)),
                   pl.BlockSpec((ROW_TILE, kvw), lambda i: (i, 0))],
        out_shape=[jax.ShapeDtypeStruct((M, W_GROUP), BF16), jax.ShapeDtypeStruct((M, kvw), BF16)],
        compiler_params=_cparams(("parallel",)),
        name="qk_prep",
    )(P, P, cos, sin, qg.reshape(1, HEAD_DIM), kg.reshape(1, HEAD_DIM))


def _attn_kernel(*refs, tq, rows, n_sets, kb):
    q_ref, kv_refs, z_ref, o_ref = refs[0], refs[1:1 + 2 * n_sets], refs[1 + 2 * n_sets], refs[2 + 2 * n_sets]
    for j in range(Q_PER_KV):
        cs = slice(j * HEAD_DIM, (j + 1) * HEAD_DIM)
        for r in range(tq // rows):
            rs = slice(r * rows, (r + 1) * rows)
            q = q_ref[rs, cs]
            m = l = acc = None
            for t in range(n_sets):
                k_ref, v_ref = kv_refs[2 * t], kv_refs[2 * t + 1]
                for c0 in range(0, k_ref.shape[0], kb):
                    ks = slice(c0, min(c0 + kb, k_ref.shape[0]))
                    s = lax.dot_general(q, k_ref[ks, :], NT_DIMS, preferred_element_type=F32)
                    bm = jnp.max(s, axis=-1, keepdims=True)
                    m_new = bm if m is None else jnp.maximum(m, bm)
                    p = jnp.exp2(s - m_new)
                    pv = jnp.dot(p.astype(BF16), v_ref[ks, :], preferred_element_type=F32)
                    ps = jnp.sum(p, axis=-1, keepdims=True)
                    if m is None:
                        l, acc = ps, pv
                    else:
                        alpha = jnp.exp2(m - m_new)
                        l, acc = alpha * l + ps, alpha * acc + pv
                    m = m_new
            o_ref[rs, cs] = (acc / l * _silu(z_ref[rs, cs].astype(F32))).astype(BF16)


def _attend(qr, kr, P, B, *, tq, rows, kb, q_block0, n_q, key_sets):
    gw = Q_PER_KV * HEAD_DIM
    qrow = lambda b, g, i: q_block0 + b * (n_q // tq) + i
    in_specs = [pl.BlockSpec((tq, gw), lambda b, g, i: (qrow(b, g, i), g))]
    args = [qr]
    for n_k, k_block0 in key_sets:
        in_specs.append(pl.BlockSpec((n_k, HEAD_DIM), lambda b, g, i, o=k_block0: (o + b, g)))
        in_specs.append(pl.BlockSpec((n_k, HEAD_DIM), lambda b, g, i, o=k_block0: (o + b, OFF_C_V // HEAD_DIM + g)))
        args += [kr, P]
    in_specs.append(pl.BlockSpec((tq, gw), lambda b, g, i: (qrow(b, g, i), OFF_C_Z // gw + g)))
    args.append(P)
    return pl.pallas_call(
        functools.partial(_attn_kernel, tq=tq, rows=rows, n_sets=len(key_sets), kb=kb),
        grid=(B, KV_HEADS, n_q // tq),
        in_specs=in_specs,
        out_specs=pl.BlockSpec((tq, gw), lambda b, g, i: (b * (n_q // tq) + i, g)),
        out_shape=jax.ShapeDtypeStruct((B * n_q, W_GROUP), BF16),
        compiler_params=_cparams(("parallel", "parallel", "parallel"), V7X_VMEM_LIMIT),
        name=f"attn_q{n_q}",
    )(*args)


def _log_sigmoid(x):
    return jnp.minimum(x, 0.0) - jnp.log1p(jnp.exp(-jnp.abs(x)))


def _split3(x):
    hi = x.astype(BF16)
    r = x - hi.astype(F32)
    mid = r.astype(BF16)
    lo = (r - mid.astype(F32)).astype(BF16)
    return hi, mid, lo


def _mlstm_dir(direction, q_refs, k_refs, v_refs, g_ref, gt_ref, brow_ref, bcol_ref, o_ref, cext_ref, m_ref):
    rev = direction == 1
    end = 0 if rev else CHUNK - 1
    scale = HEAD_DIM ** -0.5
    row = lax.broadcasted_iota(jnp.int32, (CHUNK, CHUNK), 0)
    colm = lax.broadcasted_iota(jnp.int32, (CHUNK, CHUNK), 1)
    allowed = (colm >= row) if rev else (colm <= row)
    tri = jnp.where(allowed, 1.0, 0.0).astype(BF16)

    pre = g_ref[...] + brow_ref[...]
    pre_t = gt_ref[...] + bcol_ref[...]
    b = sum(jnp.dot(tri, part, preferred_element_type=F32) for part in _split3(_log_sigmoid(pre)))
    b_t = sum(lax.dot_general(part, tri, NT_DIMS, preferred_element_type=F32)
              for part in _split3(_log_sigmoid(pre_t)))

    hph = N_HEADS // 2
    for hd in range(N_HEADS):
        cs = slice((hd % hph) * HEAD_DIM, (hd % hph + 1) * HEAD_DIM)
        st = direction * N_HEADS + hd
        ic = direction * 2 * N_HEADS + hd
        fc = ic + N_HEADS
        b_col, i_col = b[:, fc:fc + 1], pre[:, ic:ic + 1]
        a_row = pre_t[ic:ic + 1, :] - b_t[fc:fc + 1, :]
        dm = jnp.where(allowed, b_col + a_row, -jnp.inf)
        inter = b_col + m_ref[st]
        m_row = jnp.maximum(inter, jnp.max(dm, axis=-1, keepdims=True))
        w_intra = jnp.exp(dm - m_row)
        w_inter = jnp.exp(inter - m_row) * scale
        qh, kh, vh = q_refs[hd // hph][:, cs], k_refs[hd // hph][:, cs], v_refs[hd // hph][:, cs]
        s = lax.dot_general(qh, kh, NT_DIMS, preferred_element_type=F32) * (w_intra * scale)
        qc = lax.dot_general(qh, cext_ref[st].astype(BF16), NT_DIMS, preferred_element_type=F32)
        num = jnp.dot(s.astype(BF16), vh, preferred_element_type=F32) + w_inter * qc[:, 0:HEAD_DIM]
        den = jnp.sum(s, axis=-1, keepdims=True) + w_inter * qc[:, HEAD_DIM:HEAD_DIM + 1]
        o_ref[:, hd * HEAD_DIM:(hd + 1) * HEAD_DIM] = num / jnp.maximum(jnp.abs(den), jnp.exp(-m_row))

        m_end = m_row[end:end + 1, :]
        w_end = jnp.exp(b_col[end:end + 1, :] - b_col + i_col - m_end)
        decay = jnp.exp(inter[end:end + 1, :] - m_end)
        lhs = jnp.concatenate([(w_end * vh.astype(F32)).astype(BF16),
                               jnp.broadcast_to(w_end, (CHUNK, HEAD_DIM)).astype(BF16)], axis=1)
        upd = lax.dot_general(lhs, kh, (((0,), (0,)), ((), ())), preferred_element_type=F32)
        cext_ref[st] = decay * cext_ref[st] + upd
        m_ref[st] = m_end


def _mlstm_kernel(*refs):
    brow_ref, bcol_ref = refs[16], refs[17]
    of_ref, ob_ref, cext_ref, m_ref = refs[18:22]

    @pl.when(pl.program_id(1) == 0)
    def _():
        cext_ref[...] = jnp.zeros_like(cext_ref)
        m_ref[...] = jnp.zeros_like(m_ref)

    for d, o_ref in ((0, of_ref), (1, ob_ref)):
        r = refs[8 * d:8 * d + 8]
        _mlstm_dir(d, r[0:2], r[2:4], r[4:6], r[6], r[7], brow_ref, bcol_ref, o_ref, cext_ref, m_ref)


def _mlstm(P, gates, gates_t, bias_row, bias_col, B, seq, ctx_len):
    M = P.shape[0]
    ncl, ncc = seq // CHUNK, ctx_len // CHUNK
    nc = ncl + ncc
    ctx0 = B * ncl
    fwd = lambda b, j: jnp.where(j < ncc, ctx0 + b * ncc + j, b * ncl + j - ncc)
    bwd = lambda b, j: jnp.where(j < ncc, ctx0 + b * ncc + ncc - 1 - j, b * ncl + nc - 1 - j)
    in_specs, args = [], []
    for chunk in (fwd, bwd):
        for off in (OFF_D_Q, OFF_D_K, OFF_D_V):
            for half in range(2):
                in_specs.append(pl.BlockSpec((CHUNK, HALF), lambda b, j, c=chunk, o=off // HALF + half: (c(b, j), o)))
                args.append(P)
        in_specs.append(pl.BlockSpec((CHUNK, GATE_PAD), lambda b, j, c=chunk: (c(b, j), 0)))
        in_specs.append(pl.BlockSpec((GATE_PAD, CHUNK), lambda b, j, c=chunk: (0, c(b, j))))
        args += [gates, gates_t]
    in_specs += [pl.BlockSpec((1, GATE_PAD), lambda b, j: (0, 0)), pl.BlockSpec((GATE_PAD, 1), lambda b, j: (0, 0))]
    args += [bias_row, bias_col]
    return pl.pallas_call(
        _mlstm_kernel,
        grid=(B, nc),
        in_specs=in_specs,
        out_specs=[pl.BlockSpec((CHUNK, W_GROUP), lambda b, j: (fwd(b, j), 0)),
                   pl.BlockSpec((CHUNK, W_GROUP), lambda b, j: (bwd(b, j), 0))],
        out_shape=[jax.ShapeDtypeStruct((M, W_GROUP), F32), jax.ShapeDtypeStruct((M, W_GROUP), F32)],
        scratch_shapes=[pltpu.VMEM((N_DIR * N_HEADS, 2 * HEAD_DIM, HEAD_DIM), F32),
                        pltpu.VMEM((N_DIR * N_HEADS, 1, 1), F32)],
        compiler_params=_cparams(("parallel", "arbitrary")),
        name="mlstm",
    )(*args)


def _memout_kernel(hf_ref, hb_ref, o_ref, z_ref, g_ref, out_ref):
    for hd in range(HALF // HEAD_DIM):
        cs = slice(hd * HEAD_DIM, (hd + 1) * HEAD_DIM)
        h = hf_ref[:, cs] + hb_ref[:, cs]
        hn = h * lax.rsqrt(jnp.mean(h * h, axis=-1, keepdims=True) + EPS) * g_ref[:, cs]
        out_ref[:, cs] = (hn * jax.nn.sigmoid(o_ref[:, cs].astype(F32)) * _silu(z_ref[:, cs].astype(F32))).astype(BF16)


def _mem_out(hf, hb, P, n_rows, g, rows=512):
    blk = lambda off: pl.BlockSpec((rows, HALF), lambda i, hh: (i, off // HALF + hh))
    return pl.pallas_call(
        _memout_kernel,
        grid=(n_rows // rows, 2),
        in_specs=[blk(0), blk(0), blk(OFF_D_O), blk(OFF_D_Z), pl.BlockSpec((1, HALF), lambda i, hh: (0, hh))],
        out_specs=blk(0),
        out_shape=jax.ShapeDtypeStruct((P.shape[0], W_GROUP), BF16),
        compiler_params=_cparams(("parallel", "parallel")),
        name="mem_out",
    )(hf, hb, P, P, g.reshape(1, W_GROUP))


def _out_kernel(a_ref, b_ref, c_ref, d_ref, w_ref, res_ref, gate_ref, o_ref):
    x = jnp.concatenate([a_ref[...], b_ref[...], c_ref[...], d_ref[...]], axis=1)
    o_ref[...] = res_ref[...] + gate_ref[...] * jnp.dot(x, w_ref[...], preferred_element_type=F32)


def _out_proj(a, b, att, mem, w_all, layer, res, mod, *, row_block0, gate_row, tm, tn):
    n_rows, D = res.shape
    gcol = 2 * D // tn
    shifted = pl.BlockSpec((tm, W_GROUP), lambda n, m: (row_block0 + m, 0))
    return pl.pallas_call(
        _out_kernel,
        grid=(D // tn, n_rows // tm),
        in_specs=[shifted, shifted, pl.BlockSpec((tm, W_GROUP), lambda n, m: (m, 0)), shifted,
                  pl.BlockSpec((None, 4 * W_GROUP, tn), lambda n, m: (layer, 0, n)),
                  pl.BlockSpec((tm, tn), lambda n, m: (m, n)),
                  pl.BlockSpec((None, 1, tn), lambda n, m: (gate_row(m), 0, gcol + n))],
        out_specs=pl.BlockSpec((tm, tn), lambda n, m: (m, n)),
        out_shape=jax.ShapeDtypeStruct((n_rows, D), F32),
        compiler_params=_cparams(("parallel", "parallel"), V7X_VMEM_LIMIT),
        name="out_proj",
    )(a, b, att, mem, w_all, res, mod)


def kernel(x, c, ctx, c_ctx, w_ada, b_ada, norm_g, w_in, sgu_w, sgu_b, sgu_ln_g, sgu_ln_b, conv_w, conv_b,
           conv_ln_g, conv_ln_b, q_norm_g, k_norm_g, mlstm_i_bias, mlstm_f_bias, mh_norm_g, w_out):
    B, seq, D = x.shape
    ctx_len = ctx.shape[1]
    depth = w_ada.shape[0]
    ML, MC = B * seq, B * ctx_len
    M = ML + MC
    assert ctx_len == ROW_TILE and seq % 512 == 0 and M % 512 == 0 and D == 4 * W_GROUP and B < 8
    assert w_in.shape[2] == P_MAIN + GATE_COLS

    h_lat, h_ctx = x.reshape(ML, D), ctx.reshape(MC, D)
    cc = jnp.zeros((8, D), F32).at[:B].set(c).at[B].set(c_ctx)
    mod_all = _ada_mod(cc, w_ada, b_ada)
    cos, sin = _rope_tables(seq)
    w_in_b, w_out_b, sgu_w_b = w_in.astype(BF16), w_out.astype(BF16), sgu_w.astype(BF16)
    tm_in = M // 8

    for l in range(depth):
        last = l == depth - 1
        n_rows = ML if last else M
        wg = jnp.pad(w_in_b[l, :, P_MAIN:], ((0, 0), (0, GATE_PAD - GATE_COLS)))
        mod = mod_all[l].reshape(8, 1, 3 * D)
        hn, gates, gates_t = _norm_mod(h_lat, h_ctx, mod, norm_g[l], wg, wg.T, B, seq)
        P = _in_proj(hn, w_in_b, l, tm=tm_in, tn=1152)

        a_out = _gmlp(P, n_rows, sgu_w_b[l], sgu_b[l].reshape(N_HEADS, CHUNK, 1), sgu_ln_g[l], sgu_ln_b[l])
        b_out = _conv(P, n_rows, ML, seq, conv_w[l], conv_b[l], conv_ln_g[l], conv_ln_b[l])

        qr, kr = _qk_prep(P, cos, sin, q_norm_g[l], k_norm_g[l], ML, seq)
        att_lat = _attend(qr, kr, P, B, tq=256, rows=128, kb=seq, q_block0=0, n_q=seq,
                          key_sets=((seq, 0), (ctx_len, ML // ctx_len)))

        bias = jnp.concatenate([mlstm_i_bias[l], mlstm_f_bias[l]], axis=-1).reshape(GATE_COLS)
        bias = jnp.pad(bias, (0, GATE_PAD - GATE_COLS))
        hf, hb = _mlstm(P, gates, gates_t, bias.reshape(1, GATE_PAD), bias.reshape(GATE_PAD, 1), B, seq, ctx_len)
        mem = _mem_out(hf, hb, P, n_rows, mh_norm_g[l])

        if not last:
            att_ctx = _attend(qr, kr, P, B, tq=ctx_len, rows=128, kb=ctx_len, q_block0=ML // ctx_len, n_q=ctx_len,
                              key_sets=((ctx_len, ML // ctx_len),))
            h_ctx = _out_proj(a_out, b_out, att_ctx, mem, w_out_b, l, h_ctx, mod, row_block0=ML // MC,
                              gate_row=lambda m: B, tm=MC, tn=1024)
        h_lat = _out_proj(a_out, b_out, att_lat, mem, w_out_b, l, h_lat, mod, row_block0=0,
                          gate_row=lambda m: (m * 512) // seq, tm=512, tn=1024)

    return h_lat.reshape(B, seq, D)
```

```python
import functools
import math

import jax
import jax.numpy as jnp
import numpy as np
from jax import lax
from jax.experimental import pallas as pl
from jax.experimental.pallas import tpu as pltpu

F32 = jnp.float32
BF16 = jnp.bfloat16

HEAD_DIM = 128
N_HEADS = 8
KV_HEADS = 2
Q_PER_KV = N_HEADS // KV_HEADS
W_GROUP = N_HEADS * HEAD_DIM
HALF = W_GROUP // 2
CHUNK = 128
CONV_WIDTH = 31
CONV_PAD = CONV_WIDTH // 2
GRID_W = 64
ROPE_THETA = 10000.0
ROPE_PAIRS = HEAD_DIM // 4
N_DIR = 2
EPS = 1e-6
GATE_COLS = N_DIR * 2 * N_HEADS
GATE_PAD = 128

OFF_A_U, OFF_A_V, OFF_A_Z = 0, W_GROUP, 2 * W_GROUP
OFF_B_A, OFF_B_G, OFF_B_Z = 3 * W_GROUP, 4 * W_GROUP, 5 * W_GROUP
OFF_C_Q = 6 * W_GROUP
OFF_C_K = 7 * W_GROUP
OFF_C_V = OFF_C_K + KV_HEADS * HEAD_DIM
OFF_C_Z = OFF_C_V + KV_HEADS * HEAD_DIM
OFF_D_Q = OFF_C_Z + W_GROUP
OFF_D_K, OFF_D_V, OFF_D_O, OFF_D_Z = (OFF_D_Q + i * W_GROUP for i in range(1, 5))
P_MAIN = OFF_D_Z + W_GROUP

ROW_TILE = 256
HALO = 16
V7X_VMEM_LIMIT = 56 * 1024 * 1024
NT_DIMS = (((1,), (1,)), ((), ()))


def _silu(x):
    return x * jax.nn.sigmoid(x)


def _cparams(sem, vmem=None):
    return pltpu.CompilerParams(dimension_semantics=sem, vmem_limit_bytes=vmem)


def _ada_kernel(c_ref, w_ref, b_ref, o_ref):
    act = _silu(c_ref[...])
    o_ref[...] = jnp.dot(act.astype(BF16), w_ref[...].astype(BF16), preferred_element_type=F32) + b_ref[...]


def _ada_mod(cc, w_ada, b_ada, tn=512):
    L, D, N = w_ada.shape
    return pl.pallas_call(
        _ada_kernel,
        grid=(L, N // tn),
        in_specs=[pl.BlockSpec((8, D), lambda l, n: (0, 0)),
                  pl.BlockSpec((None, D, tn), lambda l, n: (l, 0, n)),
                  pl.BlockSpec((None, 1, tn), lambda l, n: (l, 0, n))],
        out_specs=pl.BlockSpec((None, 8, tn), lambda l, n: (l, 0, n)),
        out_shape=jax.ShapeDtypeStruct((L, 8, N), F32),
        compiler_params=_cparams(("parallel", "parallel")),
        name="ada_mod",
    )(cc, w_ada, b_ada.reshape(L, 1, N))


def _norm_kernel(x_ref, c_ref, mod_ref, g_ref, wg_ref, wgt_ref, hn_ref, gates_ref, gatest_ref, *, D, n_lat):
    x = jnp.where(pl.program_id(0) >= n_lat, c_ref[...], x_ref[...])
    y = x * lax.rsqrt(jnp.mean(x * x, axis=-1, keepdims=True) + EPS) * g_ref[...]
    hb = (y * (1.0 + mod_ref[:, D:2 * D]) + mod_ref[:, 0:D]).astype(BF16)
    hn_ref[...] = hb
    gates_ref[...] = jnp.dot(hb, wg_ref[...], preferred_element_type=F32)
    gatest_ref[...] = lax.dot_general(wgt_ref[...], hb, NT_DIMS, preferred_element_type=F32)


def _norm_mod(h_lat, h_ctx, mod, norm_g, wg, wgt, B, seq):
    D = h_lat.shape[1]
    n_lat = h_lat.shape[0] // ROW_TILE
    n_tiles = n_lat + h_ctx.shape[0] // ROW_TILE
    M = n_tiles * ROW_TILE
    tps = seq // ROW_TILE
    return pl.pallas_call(
        functools.partial(_norm_kernel, D=D, n_lat=n_lat),
        grid=(n_tiles,),
        in_specs=[pl.BlockSpec((ROW_TILE, D), lambda i: (jnp.minimum(i, n_lat - 1), 0)),
                  pl.BlockSpec((ROW_TILE, D), lambda i: (jnp.maximum(i - n_lat, 0), 0)),
                  pl.BlockSpec((None, 1, 3 * D), lambda i: (jnp.where(i < n_lat, i // tps, B), 0, 0)),
                  pl.BlockSpec((1, D), lambda i: (0, 0)),
                  pl.BlockSpec((D, GATE_PAD), lambda i: (0, 0)),
                  pl.BlockSpec((GATE_PAD, D), lambda i: (0, 0))],
        out_specs=[pl.BlockSpec((ROW_TILE, D), lambda i: (i, 0)),
                   pl.BlockSpec((ROW_TILE, GATE_PAD), lambda i: (i, 0)),
                   pl.BlockSpec((GATE_PAD, ROW_TILE), lambda i: (0, i))],
        out_shape=[jax.ShapeDtypeStruct((M, D), BF16),
                   jax.ShapeDtypeStruct((M, GATE_PAD), F32),
                   jax.ShapeDtypeStruct((GATE_PAD, M), F32)],
        compiler_params=_cparams(("parallel",)),
        name="norm_mod",
    )(h_lat, h_ctx, mod, norm_g.reshape(1, D), wg, wgt)


def _mm_kernel(x_ref, w_ref, o_ref, wb_ref):
    @pl.when(pl.program_id(1) == 0)
    def _():
        wb_ref[...] = w_ref[...].astype(BF16)

    o_ref[...] = jnp.dot(x_ref[...], wb_ref[...], preferred_element_type=F32).astype(o_ref.dtype)


def _in_proj(hn, w_all, layer, tm, tn):
    M, K = hn.shape
    return pl.pallas_call(
        _mm_kernel,
        grid=(P_MAIN // tn, M // tm),
        in_specs=[pl.BlockSpec((tm, K), lambda n, m: (m, 0)),
                  pl.BlockSpec((None, K, tn), lambda n, m: (layer, 0, n))],
        out_specs=pl.BlockSpec((tm, tn), lambda n, m: (m, n)),
        out_shape=jax.ShapeDtypeStruct((M, P_MAIN), BF16),
        scratch_shapes=[pltpu.VMEM((K, tn), BF16)],
        compiler_params=_cparams(("parallel", "arbitrary"), V7X_VMEM_LIMIT),
        name="in_proj",
    )(hn, w_all)


def _layernorm_rows(x, g, b):
    mu = jnp.mean(x, axis=-1, keepdims=True)
    xc = x - mu
    return xc * lax.rsqrt(jnp.mean(xc * xc, axis=-1, keepdims=True) + EPS) * g + b


def _gmlp_kernel(u_ref, v_ref, z_ref, ws_ref, bs_ref, lg_ref, lb_ref, o_ref, *, rows):
    for c in range(rows // CHUNK):
        rs = slice(c * CHUNK, (c + 1) * CHUNK)
        vn = _layernorm_rows(v_ref[rs, :].astype(F32), lg_ref[...], lb_ref[...]).astype(BF16)
        for hd in range(N_HEADS):
            cs = slice(hd * HEAD_DIM, (hd + 1) * HEAD_DIM)
            mixed = jnp.dot(ws_ref[hd], vn[:, cs], preferred_element_type=F32) + bs_ref[hd]
            u = u_ref[rs, cs].astype(F32)
            o_ref[rs, cs] = (u * mixed * _silu(z_ref[rs, cs].astype(F32))).astype(BF16)


def _gmlp(P, n_rows, ws, bs, ln_g, ln_b, rows=512):
    col = lambda off: pl.BlockSpec((rows, W_GROUP), lambda i: (i, off // W_GROUP))
    return pl.pallas_call(
        functools.partial(_gmlp_kernel, rows=rows),
        grid=(n_rows // rows,),
        in_specs=[col(OFF_A_U), col(OFF_A_V), col(OFF_A_Z),
                  pl.BlockSpec((N_HEADS, CHUNK, CHUNK), lambda i: (0, 0, 0)),
                  pl.BlockSpec((N_HEADS, CHUNK, 1), lambda i: (0, 0, 0)),
                  pl.BlockSpec((1, W_GROUP), lambda i: (0, 0)),
                  pl.BlockSpec((1, W_GROUP), lambda i: (0, 0))],
        out_specs=pl.BlockSpec((rows, W_GROUP), lambda i: (i, 0)),
        out_shape=jax.ShapeDtypeStruct((n_rows, W_GROUP), BF16),
        compiler_params=_cparams(("parallel",)),
        name="gmlp",
    )(P, P, P, ws, bs, ln_g.reshape(1, W_GROUP), ln_b.reshape(1, W_GROUP))


def _conv_kernel(a_ref, g_ref, z_ref, ap_ref, gp_ref, an_ref, gn_ref, w_ref, cb_ref, lg_ref, lb_ref, o_ref,
                 yext_ref, ysh_ref, cbuf_ref, *, tps, n_lat):
    i = pl.program_id(0)
    is_lat = i < n_lat
    has_left = jnp.logical_and(is_lat, i % tps != 0)
    has_right = jnp.logical_and(is_lat, i % tps != tps - 1)
    glu = lambda a, g: a.astype(F32) * jax.nn.sigmoid(g.astype(F32))
    yext_ref[HALO:HALO + ROW_TILE, :] = glu(a_ref[...], g_ref[...])
    yext_ref[0:HALO, :] = jnp.where(has_left, glu(ap_ref[...], gp_ref[...]), 0.0)
    yext_ref[HALO + ROW_TILE:, :] = jnp.where(has_right, glu(an_ref[...], gn_ref[...]), 0.0)
    n_sh = ysh_ref.shape[1]
    for r in range(1, 8):
        ysh_ref[r - 1] = yext_ref[r:r + n_sh, :]
    off = HALO - CONV_PAD
    for c in range(W_GROUP // 128):
        cs = slice(c * 128, (c + 1) * 128)
        acc = jnp.zeros((ROW_TILE, 128), F32)
        for k in range(CONV_WIDTH):
            q8, r = 8 * ((off + k) // 8), (off + k) % 8
            tap = yext_ref[q8:q8 + ROW_TILE, cs] if r == 0 else ysh_ref[r - 1, q8:q8 + ROW_TILE, cs]
            acc = acc + tap * w_ref[k:k + 1, cs]
        cbuf_ref[:, cs] = acc + cb_ref[:, cs]
    y = _layernorm_rows(cbuf_ref[...], lg_ref[...], lb_ref[...])
    o_ref[...] = (_silu(y) * _silu(z_ref[...].astype(F32))).astype(BF16)


def _conv(P, n_rows, n_lat_rows, seq, conv_w, conv_b, ln_g, ln_b):
    hpt = ROW_TILE // HALO
    nhalo = P.shape[0] // HALO
    col = lambda off: pl.BlockSpec((ROW_TILE, W_GROUP), lambda i: (i, off // W_GROUP))
    prev = lambda off: pl.BlockSpec((HALO, W_GROUP), lambda i: (jnp.maximum(i * hpt - 1, 0), off // W_GROUP))
    nxt = lambda off: pl.BlockSpec((HALO, W_GROUP),
                                   lambda i: (jnp.minimum((i + 1) * hpt, nhalo - 1), off // W_GROUP))
    vec = pl.BlockSpec((1, W_GROUP), lambda i: (0, 0))
    return pl.pallas_call(
        functools.partial(_conv_kernel, tps=seq // ROW_TILE, n_lat=n_lat_rows // ROW_TILE),
        grid=(n_rows // ROW_TILE,),
        in_specs=[col(OFF_B_A), col(OFF_B_G), col(OFF_B_Z), prev(OFF_B_A), prev(OFF_B_G), nxt(OFF_B_A),
                  nxt(OFF_B_G), pl.BlockSpec((CONV_WIDTH, W_GROUP), lambda i: (0, 0)), vec, vec, vec],
        out_specs=pl.BlockSpec((ROW_TILE, W_GROUP), lambda i: (i, 0)),
        out_shape=jax.ShapeDtypeStruct((n_rows, W_GROUP), BF16),
        scratch_shapes=[pltpu.VMEM((ROW_TILE + 2 * HALO, W_GROUP), F32),
                        pltpu.VMEM((7, ROW_TILE + 2 * HALO - 8, W_GROUP), F32),
                        pltpu.VMEM((ROW_TILE, W_GROUP), F32)],
        compiler_params=_cparams(("parallel",)),
        name="conv",
    )(P, P, P, P, P, P, P, conv_w, conv_b.reshape(1, W_GROUP), ln_g.reshape(1, W_GROUP),
      ln_b.reshape(1, W_GROUP))


def _rope_tables(seq):
    t = jnp.arange(seq)
    pos = jnp.stack([t // GRID_W, t % GRID_W], axis=-1).astype(F32)
    freq = ROPE_THETA ** (-jnp.arange(ROPE_PAIRS, dtype=F32) / ROPE_PAIRS)
    ang = pos[:, :, None] * freq
    cos, sin = jnp.cos(ang), jnp.sin(ang)
    cfull = jnp.concatenate([cos[:, 0], cos[:, 0], cos[:, 1], cos[:, 1]], axis=-1)
    sfull = jnp.concatenate([-sin[:, 0], sin[:, 0], -sin[:, 1], sin[:, 1]], axis=-1)
    cfull = jnp.concatenate([cfull, jnp.ones((ROW_TILE, HEAD_DIM), F32)], axis=0)
    sfull = jnp.concatenate([sfull, jnp.zeros((ROW_TILE, HEAD_DIM), F32)], axis=0)
    return cfull, sfull


def _norm_rope(x, g, cos, sin, swap_lo):
    y = x * lax.rsqrt(jnp.mean(x * x, axis=-1, keepdims=True) + EPS) * g
    partner = jnp.where(swap_lo, pltpu.roll(y, HEAD_DIM - ROPE_PAIRS, axis=1), pltpu.roll(y, ROPE_PAIRS, axis=1))
    return y * cos + partner * sin


def _qk_kernel(q_ref, k_ref, cos_ref, sin_ref, qg_ref, kg_ref, qo_ref, ko_ref):
    cos, sin = cos_ref[...], sin_ref[...]
    lane = lax.broadcasted_iota(jnp.int32, (ROW_TILE, HEAD_DIM), 1)
    swap_lo = (lane & ROPE_PAIRS) == 0
    q_scale = HEAD_DIM ** -0.5 * math.log2(math.e)
    for hd in range(N_HEADS):
        cs = slice(hd * HEAD_DIM, (hd + 1) * HEAD_DIM)
        r = _norm_rope(q_ref[:, cs].astype(F32), qg_ref[...], cos, sin, swap_lo)
        qo_ref[:, cs] = (r * q_scale).astype(BF16)
    for hd in range(KV_HEADS):
        cs = slice(hd * HEAD_DIM, (hd + 1) * HEAD_DIM)
        ko_ref[:, cs] = _norm_rope(k_ref[:, cs].astype(F32), kg_ref[...], cos, sin, swap_lo).astype(BF16)


def _qk_prep(P, cos, sin, qg, kg, n_lat_rows, seq):
    M = P.shape[0]
    tps = seq // ROW_TILE
    n_lat = n_lat_rows // ROW_TILE
    kvw = KV_HEADS * HEAD_DIM
    tab = pl.BlockSpec((ROW_TILE, HEAD_DIM), lambda i: (jnp.where(i < n_lat, i % tps, tps), 0))
    vec = pl.BlockSpec((1, HEAD_DIM), lambda i: (0, 0))
    return pl.pallas_call(
        _qk_kernel,
        grid=(M // ROW_TILE,),
        in_specs=[pl.BlockSpec((ROW_TILE, W_GROUP), lambda i: (i, OFF_C_Q // W_GROUP)),
                  pl.BlockSpec((ROW_TILE, kvw), lambda i: (i, OFF_C_K // kvw)), tab, tab, vec, vec],
        out_specs=[pl.BlockSpec((ROW_TILE, W_GROUP), lambda i: (i, 0)),
                   pl.BlockSpec((ROW_TILE, kvw), lambda i: (i, 0))],
        out_shape=[jax.ShapeDtypeStruct((M, W_GROUP), BF16), jax.ShapeDtypeStruct((M, kvw), BF16)],
        compiler_params=_cparams(("parallel",)),
        name="qk_prep",
    )(P, P, cos, sin, qg.reshape(1, HEAD_DIM), kg.reshape(1, HEAD_DIM))


def _attn_kernel(*refs, tq, rows, n_sets, kb):
    q_ref, kv_refs, z_ref, o_ref = refs[0], refs[1:1 + 2 * n_sets], refs[1 + 2 * n_sets], refs[2 + 2 * n_sets]
    for j in range(Q_PER_KV):
        cs = slice(j * HEAD_DIM, (j + 1) * HEAD_DIM)
        for r in range(tq // rows):
            rs = slice(r * rows, (r + 1) * rows)
            q = q_ref[rs, cs]
            m = l = acc = None
            for t in range(n_sets):
                k_ref, v_ref = kv_refs[2 * t], kv_refs[2 * t + 1]
                for c0 in range(0, k_ref.shape[0], kb):
                    ks = slice(c0, min(c0 + kb, k_ref.shape[0]))
                    s = lax.dot_general(q, k_ref[ks, :], NT_DIMS, preferred_element_type=F32)
                    bm = jnp.max(s, axis=-1, keepdims=True)
                    m_new = bm if m is None else jnp.maximum(m, bm)
                    p = jnp.exp2(s - m_new)
                    pv = jnp.dot(p.astype(BF16), v_ref[ks, :], preferred_element_type=F32)
                    ps = jnp.sum(p, axis=-1, keepdims=True)
                    if m is None:
                        l, acc = ps, pv
                    else:
                        alpha = jnp.exp2(m - m_new)
                        l, acc = alpha * l + ps, alpha * acc + pv
                    m = m_new
            o_ref[rs, cs] = (acc / l * _silu(z_ref[rs, cs].astype(F32))).astype(BF16)


def _attend(qr, kr, P, B, *, tq, rows, kb, q_block0, n_q, key_sets):
    gw = Q_PER_KV * HEAD_DIM
    qrow = lambda b, g, i: q_block0 + b * (n_q // tq) + i
    in_specs = [pl.BlockSpec((tq, gw), lambda b, g, i: (qrow(b, g, i), g))]
    args = [qr]
    for n_k, k_block0 in key_sets:
        in_specs.append(pl.BlockSpec((n_k, HEAD_DIM), lambda b, g, i, o=k_block0: (o + b, g)))
        in_specs.append(pl.BlockSpec((n_k, HEAD_DIM), lambda b, g, i, o=k_block0: (o + b, OFF_C_V // HEAD_DIM + g)))
        args += [kr, P]
    in_specs.append(pl.BlockSpec((tq, gw), lambda b, g, i: (qrow(b, g, i), OFF_C_Z // gw + g)))
    args.append(P)
    return pl.pallas_call(
        functools.partial(_attn_kernel, tq=tq, rows=rows, n_sets=len(key_sets), kb=kb),
        grid=(B, KV_HEADS, n_q // tq),
        in_specs=in_specs,
        out_specs=pl.BlockSpec((tq, gw), lambda b, g, i: (b * (n_q // tq) + i, g)),
        out_shape=jax.ShapeDtypeStruct((B * n_q, W_GROUP), BF16),
        compiler_params=_cparams(("parallel", "parallel", "parallel"), V7X_VMEM_LIMIT),
        name=f"attn_q{n_q}",
    )(*args)


def _log_sigmoid(x):
    return jnp.minimum(x, 0.0) - jnp.log1p(jnp.exp(-jnp.abs(x)))


def _split3(x):
    hi = x.astype(BF16)
    r = x - hi.astype(F32)
    mid = r.astype(BF16)
    lo = (r - mid.astype(F32)).astype(BF16)
    return hi, mid, lo


def _mlstm_dir(direction, q_refs, k_refs, v_refs, g_ref, gt_ref, brow_ref, bcol_ref, sel_ref, o_ref, cext_ref, m_ref):
    rev = direction == 1
    end = 0 if rev else CHUNK - 1
    scale = HEAD_DIM ** -0.5
    row = lax.broadcasted_iota(jnp.int32, (CHUNK, CHUNK), 0)
    colm = lax.broadcasted_iota(jnp.int32, (CHUNK, CHUNK), 1)
    allowed = (colm >= row) if rev else (colm <= row)
    tri = jnp.where(allowed, 1.0, 0.0).astype(BF16)
    eye = jnp.where(colm == row, 1.0, 0.0).astype(BF16)

    pre_t = gt_ref[...] + bcol_ref[...]
    lf = _log_sigmoid(g_ref[...] + brow_ref[...])
    b = sum(jnp.dot(tri, part, preferred_element_type=F32) for part in _split3(lf))
    b_t = sum(lax.dot_general(part, tri, NT_DIMS, preferred_element_type=F32)
              for part in _split3(_log_sigmoid(pre_t)))
    bc_all = jnp.dot(jnp.concatenate(_split3(b), axis=1), sel_ref[direction], preferred_element_type=F32)

    hph = N_HEADS // 2
    for hd in range(N_HEADS):
        cs = slice((hd % hph) * HEAD_DIM, (hd % hph + 1) * HEAD_DIM)
        hs = slice(hd * HEAD_DIM, (hd + 1) * HEAD_DIM)
        st = direction * N_HEADS + hd
        ic = direction * 2 * N_HEADS + hd
        fc = ic + N_HEADS
        a_row = pre_t[ic:ic + 1, :] - b_t[fc:fc + 1, :]
        a_mask = jnp.where(allowed, a_row, -jnp.inf)
        m_prev = m_ref[st]
        g = jnp.maximum(m_prev, jnp.max(a_mask, axis=-1, keepdims=True))
        w_intra = jnp.exp(a_mask - g)
        w_inter = jnp.exp(m_prev - g) * scale
        bc = bc_all[:, hs]
        m_row = bc + g
        qh, kh, vh = q_refs[hd // hph][:, cs], k_refs[hd // hph][:, cs], v_refs[hd // hph][:, cs]
        s = lax.dot_general(qh, kh, NT_DIMS, preferred_element_type=F32) * (w_intra * scale)
        qc = lax.dot_general(qh, cext_ref[st].astype(BF16), NT_DIMS, preferred_element_type=F32)
        num = jnp.dot(s.astype(BF16), vh, preferred_element_type=F32) + w_inter * qc[:, 0:HEAD_DIM]
        den = jnp.sum(s, axis=-1, keepdims=True) + w_inter * qc[:, HEAD_DIM:]
        o_ref[:, hs] = num / jnp.maximum(jnp.abs(den), jnp.exp(-m_row))

        m_end, b_end = m_row[end:end + 1, :], bc[end:end + 1, :]
        w_end = jnp.exp(a_row + (b_end - m_end))
        decay = jnp.exp(b_end + m_prev - m_end)
        v_t = lax.dot_general(eye, vh, NT_DIMS, preferred_element_type=F32)
        lhs = jnp.concatenate([(v_t * w_end).astype(BF16),
                               jnp.broadcast_to(w_end, (HEAD_DIM, CHUNK)).astype(BF16)], axis=0)
        cext_ref[st] = decay * cext_ref[st] + jnp.dot(lhs, kh, preferred_element_type=F32)
        m_ref[st] = m_end


def _mlstm_kernel(*refs):
    brow_ref, bcol_ref, sel_ref = refs[16:19]
    of_ref, ob_ref, cext_ref, m_ref = refs[19:23]

    @pl.when(pl.program_id(1) == 0)
    def _():
        cext_ref[...] = jnp.zeros_like(cext_ref)
        m_ref[...] = jnp.zeros_like(m_ref)

    for d, o_ref in ((0, of_ref), (1, ob_ref)):
        r = refs[8 * d:8 * d + 8]
        _mlstm_dir(d, r[0:2], r[2:4], r[4:6], r[6], r[7], brow_ref, bcol_ref, sel_ref, o_ref, cext_ref, m_ref)


def _forget_selector():
    sel = np.zeros((N_DIR, 3 * GATE_PAD, W_GROUP), np.float32)
    for d in range(N_DIR):
        for hd in range(N_HEADS):
            fc = d * 2 * N_HEADS + N_HEADS + hd
            for part in range(3):
                sel[d, part * GATE_PAD + fc, hd * HEAD_DIM:(hd + 1) * HEAD_DIM] = 1.0
    return jnp.asarray(sel, BF16)


def _mlstm(P, gates, gates_t, bias_row, bias_col, B, seq, ctx_len):
    M = P.shape[0]
    ncl, ncc = seq // CHUNK, ctx_len // CHUNK
    nc = ncl + ncc
    ctx0 = B * ncl
    fwd = lambda b, j: jnp.where(j < ncc, ctx0 + b * ncc + j, b * ncl + j - ncc)
    bwd = lambda b, j: jnp.where(j < ncc, ctx0 + b * ncc + ncc - 1 - j, b * ncl + nc - 1 - j)
    in_specs, args = [], []
    for chunk in (fwd, bwd):
        for off in (OFF_D_Q, OFF_D_K, OFF_D_V):
            for half in range(2):
                in_specs.append(pl.BlockSpec((CHUNK, HALF), lambda b, j, c=chunk, o=off // HALF + half: (c(b, j), o)))
                args.append(P)
        in_specs.append(pl.BlockSpec((CHUNK, GATE_PAD), lambda b, j, c=chunk: (c(b, j), 0)))
        in_specs.append(pl.BlockSpec((GATE_PAD, CHUNK), lambda b, j, c=chunk: (0, c(b, j))))
        args += [gates, gates_t]
    in_specs += [pl.BlockSpec((1, GATE_PAD), lambda b, j: (0, 0)), pl.BlockSpec((GATE_PAD, 1), lambda b, j: (0, 0)),
                 pl.BlockSpec((N_DIR, 3 * GATE_PAD, W_GROUP), lambda b, j: (0, 0, 0))]
    args += [bias_row, bias_col, _forget_selector()]
    return pl.pallas_call(
        _mlstm_kernel,
        grid=(B, nc),
        in_specs=in_specs,
        out_specs=[pl.BlockSpec((CHUNK, W_GROUP), lambda b, j: (fwd(b, j), 0)),
                   pl.BlockSpec((CHUNK, W_GROUP), lambda b, j: (bwd(b, j), 0))],
        out_shape=[jax.ShapeDtypeStruct((M, W_GROUP), F32), jax.ShapeDtypeStruct((M, W_GROUP), F32)],
        scratch_shapes=[pltpu.VMEM((N_DIR * N_HEADS, 2 * HEAD_DIM, HEAD_DIM), F32),
                        pltpu.VMEM((N_DIR * N_HEADS, 1, HEAD_DIM), F32)],
        compiler_params=_cparams(("parallel", "arbitrary")),
        name="mlstm",
    )(*args)


def _memout_kernel(hf_ref, hb_ref, o_ref, z_ref, g_ref, out_ref):
    for hd in range(HALF // HEAD_DIM):
        cs = slice(hd * HEAD_DIM, (hd + 1) * HEAD_DIM)
        h = hf_ref[:, cs] + hb_ref[:, cs]
        hn = h * lax.rsqrt(jnp.mean(h * h, axis=-1, keepdims=True) + EPS) * g_ref[:, cs]
        out_ref[:, cs] = (hn * jax.nn.sigmoid(o_ref[:, cs].astype(F32)) * _silu(z_ref[:, cs].astype(F32))).astype(BF16)


def _mem_out(hf, hb, P, n_rows, g, rows=512):
    blk = lambda off: pl.BlockSpec((rows, HALF), lambda i, hh: (i, off // HALF + hh))
    return pl.pallas_call(
        _memout_kernel,
        grid=(n_rows // rows, 2),
        in_specs=[blk(0), blk(0), blk(OFF_D_O), blk(OFF_D_Z), pl.BlockSpec((1, HALF), lambda i, hh: (0, hh))],
        out_specs=blk(0),
        out_shape=jax.ShapeDtypeStruct((n_rows, W_GROUP), BF16),
        compiler_params=_cparams(("parallel", "parallel")),
        name="mem_out",
    )(hf, hb, P, P, g.reshape(1, W_GROUP))


def _out_kernel(a_ref, b_ref, c_ref, d_ref, w_ref, res_ref, gate_ref, o_ref):
    x = jnp.concatenate([a_ref[...], b_ref[...], c_ref[...], d_ref[...]], axis=1)
    o_ref[...] = res_ref[...] + gate_ref[...] * jnp.dot(x, w_ref[...], preferred_element_type=F32)


def _out_proj(a, b, att, mem, w_all, layer, res, mod, *, row_block0, gate_row, tm, tn):
    n_rows, D = res.shape
    gcol = 2 * D // tn
    shifted = pl.BlockSpec((tm, W_GROUP), lambda n, m: (row_block0 + m, 0))
    return pl.pallas_call(
        _out_kernel,
        grid=(D // tn, n_rows // tm),
        in_specs=[shifted, shifted, pl.BlockSpec((tm, W_GROUP), lambda n, m: (m, 0)), shifted,
                  pl.BlockSpec((None, 4 * W_GROUP, tn), lambda n, m: (layer, 0, n)),
                  pl.BlockSpec((tm, tn), lambda n, m: (m, n)),
                  pl.BlockSpec((None, 1, tn), lambda n, m: (gate_row(m), 0, gcol + n))],
        out_specs=pl.BlockSpec((tm, tn), lambda n, m: (m, n)),
        out_shape=jax.ShapeDtypeStruct((n_rows, D), F32),
        compiler_params=_cparams(("parallel", "parallel"), V7X_VMEM_LIMIT),
        name="out_proj",
    )(a, b, att, mem, w_all, res, mod)


def kernel(x, c, ctx, c_ctx, w_ada, b_ada, norm_g, w_in, sgu_w, sgu_b, sgu_ln_g, sgu_ln_b, conv_w, conv_b,
           conv_ln_g, conv_ln_b, q_norm_g, k_norm_g, mlstm_i_bias, mlstm_f_bias, mh_norm_g, w_out):
    B, seq, D = x.shape
    ctx_len = ctx.shape[1]
    depth = w_ada.shape[0]
    ML, MC = B * seq, B * ctx_len
    M = ML + MC
    assert ctx_len == ROW_TILE and seq % 512 == 0 and M % 512 == 0 and D == 4 * W_GROUP and B < 8
    assert w_in.shape[2] == P_MAIN + GATE_COLS

    h_lat, h_ctx = x.reshape(ML, D), ctx.reshape(MC, D)
    cc = jnp.zeros((8, D), F32).at[:B].set(c).at[B].set(c_ctx)
    mod_all = _ada_mod(cc, w_ada, b_ada)
    cos, sin = _rope_tables(seq)
    w_out_b, sgu_w_b = w_out.astype(BF16), sgu_w.astype(BF16)
    tm_in = M // 8

    for l in range(depth):
        last = l == depth - 1
        n_rows = ML if last else M
        wg = jnp.pad(w_in[l, :, P_MAIN:].astype(BF16), ((0, 0), (0, GATE_PAD - GATE_COLS)))
        mod = mod_all[l].reshape(8, 1, 3 * D)
        hn, gates, gates_t = _norm_mod(h_lat, h_ctx, mod, norm_g[l], wg, wg.T, B, seq)
        P = _in_proj(hn, w_in, l, tm=tm_in, tn=512)

        a_out = _gmlp(P, n_rows, sgu_w_b[l], sgu_b[l].reshape(N_HEADS, CHUNK, 1), sgu_ln_g[l], sgu_ln_b[l])
        b_out = _conv(P, n_rows, ML, seq, conv_w[l], conv_b[l], conv_ln_g[l], conv_ln_b[l])

        qr, kr = _qk_prep(P, cos, sin, q_norm_g[l], k_norm_g[l], ML, seq)
        att_lat = _attend(qr, kr, P, B, tq=512, rows=512, kb=seq, q_block0=0, n_q=seq,
                          key_sets=((seq, 0), (ctx_len, ML // ctx_len)))

        bias = jnp.concatenate([mlstm_i_bias[l], mlstm_f_bias[l]], axis=-1).reshape(GATE_COLS)
        bias = jnp.pad(bias, (0, GATE_PAD - GATE_COLS))
        hf, hb = _mlstm(P, gates, gates_t, bias.reshape(1, GATE_PAD), bias.reshape(GATE_PAD, 1), B, seq, ctx_len)
        mem = _mem_out(hf, hb, P, n_rows, mh_norm_g[l])

        if not last:
            att_ctx = _attend(qr, kr, P, B, tq=ctx_len, rows=128, kb=ctx_len, q_block0=ML // ctx_len, n_q=ctx_len,
                              key_sets=((ctx_len, ML // ctx_len),))
            h_ctx = _out_proj(a_out, b_out, att_ctx, mem, w_out_b, l, h_ctx, mod, row_block0=ML // MC,
                              gate_row=lambda m: B, tm=MC, tn=1024)
        h_lat = _out_proj(a_out, b_out, att_lat, mem, w_out_b, l, h_lat, mod, row_block0=0,
                          gate_row=lambda m: (m * 512) // seq, tm=512, tn=1024)

    return h_lat.reshape(B, seq, D)
```

```python
import functools
import math

import jax
import jax.numpy as jnp
import numpy as np
from jax import lax
from jax.experimental import pallas as pl
from jax.experimental.pallas import tpu as pltpu

F32 = jnp.float32
BF16 = jnp.bfloat16

HEAD_DIM = 128
N_HEADS = 8
KV_HEADS = 2
Q_PER_KV = N_HEADS // KV_HEADS
W_GROUP = N_HEADS * HEAD_DIM
HALF = W_GROUP // 2
CHUNK = 128
CONV_WIDTH = 31
CONV_PAD = CONV_WIDTH // 2
GRID_W = 64
ROPE_THETA = 10000.0
ROPE_PAIRS = HEAD_DIM // 4
N_DIR = 2
EPS = 1e-6
GATE_COLS = N_DIR * 2 * N_HEADS
GATE_PAD = 128

OFF_A_U, OFF_A_V, OFF_A_Z = 0, W_GROUP, 2 * W_GROUP
OFF_B_A, OFF_B_G, OFF_B_Z = 3 * W_GROUP, 4 * W_GROUP, 5 * W_GROUP
OFF_C_Q = 6 * W_GROUP
OFF_C_K = 7 * W_GROUP
OFF_C_V = OFF_C_K + KV_HEADS * HEAD_DIM
OFF_C_Z = OFF_C_V + KV_HEADS * HEAD_DIM
OFF_D_Q = OFF_C_Z + W_GROUP
OFF_D_K, OFF_D_V, OFF_D_O, OFF_D_Z = (OFF_D_Q + i * W_GROUP for i in range(1, 5))
P_MAIN = OFF_D_Z + W_GROUP

ROW_TILE = 256
HALO = 16
V7X_VMEM_LIMIT = 56 * 1024 * 1024
NT_DIMS = (((1,), (1,)), ((), ()))


def _silu(x):
    return x * jax.nn.sigmoid(x)


def _cparams(sem, vmem=None):
    return pltpu.CompilerParams(dimension_semantics=sem, vmem_limit_bytes=vmem)


def _ada_kernel(c_ref, w_ref, b_ref, o_ref):
    act = _silu(c_ref[...])
    o_ref[...] = jnp.dot(act.astype(BF16), w_ref[...].astype(BF16), preferred_element_type=F32) + b_ref[...]


def _ada_mod(cc, w_ada, b_ada, tn=512):
    L, D, N = w_ada.shape
    return pl.pallas_call(
        _ada_kernel,
        grid=(L, N // tn),
        in_specs=[pl.BlockSpec((8, D), lambda l, n: (0, 0)),
                  pl.BlockSpec((None, D, tn), lambda l, n: (l, 0, n)),
                  pl.BlockSpec((None, 1, tn), lambda l, n: (l, 0, n))],
        out_specs=pl.BlockSpec((None, 8, tn), lambda l, n: (l, 0, n)),
        out_shape=jax.ShapeDtypeStruct((L, 8, N), F32),
        compiler_params=_cparams(("parallel", "parallel")),
        name="ada_mod",
    )(cc, w_ada, b_ada.reshape(L, 1, N))


def _norm_kernel(x_ref, c_ref, mod_ref, g_ref, wg_ref, wgt_ref, hn_ref, gates_ref, gatest_ref, *, D, n_lat):
    x = jnp.where(pl.program_id(0) >= n_lat, c_ref[...], x_ref[...])
    y = x * lax.rsqrt(jnp.mean(x * x, axis=-1, keepdims=True) + EPS) * g_ref[...]
    hb = (y * (1.0 + mod_ref[:, D:2 * D]) + mod_ref[:, 0:D]).astype(BF16)
    hn_ref[...] = hb
    gates_ref[...] = jnp.dot(hb, wg_ref[...].astype(BF16), preferred_element_type=F32)
    gatest_ref[...] = lax.dot_general(wgt_ref[...].astype(BF16), hb, NT_DIMS, preferred_element_type=F32)


def _norm_mod(h_lat, h_ctx, mod, norm_g, wg, wgt, B, seq):
    D = h_lat.shape[1]
    n_lat = h_lat.shape[0] // ROW_TILE
    n_tiles = n_lat + h_ctx.shape[0] // ROW_TILE
    M = n_tiles * ROW_TILE
    tps = seq // ROW_TILE
    return pl.pallas_call(
        functools.partial(_norm_kernel, D=D, n_lat=n_lat),
        grid=(n_tiles,),
        in_specs=[pl.BlockSpec((ROW_TILE, D), lambda i: (jnp.minimum(i, n_lat - 1), 0)),
                  pl.BlockSpec((ROW_TILE, D), lambda i: (jnp.maximum(i - n_lat, 0), 0)),
                  pl.BlockSpec((None, 1, 3 * D), lambda i: (jnp.where(i < n_lat, i // tps, B), 0, 0)),
                  pl.BlockSpec((1, D), lambda i: (0, 0)),
                  pl.BlockSpec((D, GATE_PAD), lambda i: (0, 0)),
                  pl.BlockSpec((GATE_PAD, D), lambda i: (0, 0))],
        out_specs=[pl.BlockSpec((ROW_TILE, D), lambda i: (i, 0)),
                   pl.BlockSpec((ROW_TILE, GATE_PAD), lambda i: (i, 0)),
                   pl.BlockSpec((GATE_PAD, ROW_TILE), lambda i: (0, i))],
        out_shape=[jax.ShapeDtypeStruct((M, D), BF16),
                   jax.ShapeDtypeStruct((M, GATE_PAD), F32),
                   jax.ShapeDtypeStruct((GATE_PAD, M), F32)],
        compiler_params=_cparams(("parallel",)),
        name="norm_mod",
    )(h_lat, h_ctx, mod, norm_g.reshape(1, D), wg, wgt)


def _mm_kernel(x_ref, w_ref, o_ref, wb_ref):
    @pl.when(pl.program_id(1) == 0)
    def _():
        wb_ref[...] = w_ref[...].astype(BF16)

    o_ref[...] = lax.dot_general(x_ref[...], wb_ref[...], NT_DIMS, preferred_element_type=F32).astype(o_ref.dtype)


def _in_proj(hn, w_t, layer, tm, tn):
    M, K = hn.shape
    return pl.pallas_call(
        _mm_kernel,
        grid=(P_MAIN // tn, M // tm),
        in_specs=[pl.BlockSpec((tm, K), lambda n, m: (m, 0)),
                  pl.BlockSpec((None, tn, K), lambda n, m: (layer, n, 0))],
        out_specs=pl.BlockSpec((tm, tn), lambda n, m: (m, n)),
        out_shape=jax.ShapeDtypeStruct((M, P_MAIN), BF16),
        scratch_shapes=[pltpu.VMEM((tn, K), BF16)],
        compiler_params=_cparams(("parallel", "arbitrary"), V7X_VMEM_LIMIT),
        name="in_proj",
    )(hn, w_t)


def _layernorm_rows(x, g, b):
    mu = jnp.mean(x, axis=-1, keepdims=True)
    xc = x - mu
    return xc * lax.rsqrt(jnp.mean(xc * xc, axis=-1, keepdims=True) + EPS) * g + b


def _gmlp_kernel(u_ref, v_ref, z_ref, ws_ref, bs_ref, lg_ref, lb_ref, o_ref, *, rows):
    for c in range(rows // CHUNK):
        rs = slice(c * CHUNK, (c + 1) * CHUNK)
        vn = _layernorm_rows(v_ref[rs, :].astype(F32), lg_ref[...], lb_ref[...]).astype(BF16)
        for hd in range(N_HEADS):
            cs = slice(hd * HEAD_DIM, (hd + 1) * HEAD_DIM)
            mixed = jnp.dot(ws_ref[hd], vn[:, cs], preferred_element_type=F32) + bs_ref[hd]
            u = u_ref[rs, cs].astype(F32)
            o_ref[rs, cs] = (u * mixed * _silu(z_ref[rs, cs].astype(F32))).astype(BF16)


def _gmlp(P, n_rows, ws, bs, ln_g, ln_b, rows=512):
    col = lambda off: pl.BlockSpec((rows, W_GROUP), lambda i: (i, off // W_GROUP))
    return pl.pallas_call(
        functools.partial(_gmlp_kernel, rows=rows),
        grid=(n_rows // rows,),
        in_specs=[col(OFF_A_U), col(OFF_A_V), col(OFF_A_Z),
                  pl.BlockSpec((N_HEADS, CHUNK, CHUNK), lambda i: (0, 0, 0)),
                  pl.BlockSpec((N_HEADS, CHUNK, 1), lambda i: (0, 0, 0)),
                  pl.BlockSpec((1, W_GROUP), lambda i: (0, 0)),
                  pl.BlockSpec((1, W_GROUP), lambda i: (0, 0))],
        out_specs=pl.BlockSpec((rows, W_GROUP), lambda i: (i, 0)),
        out_shape=jax.ShapeDtypeStruct((n_rows, W_GROUP), BF16),
        compiler_params=_cparams(("parallel",)),
        name="gmlp",
    )(P, P, P, ws, bs, ln_g.reshape(1, W_GROUP), ln_b.reshape(1, W_GROUP))


def _conv_kernel(a_ref, g_ref, z_ref, ap_ref, gp_ref, an_ref, gn_ref, w_ref, cb_ref, lg_ref, lb_ref, o_ref,
                 yext_ref, ysh_ref, cbuf_ref, *, tps, n_lat):
    i = pl.program_id(0)
    is_lat = i < n_lat
    has_left = jnp.logical_and(is_lat, i % tps != 0)
    has_right = jnp.logical_and(is_lat, i % tps != tps - 1)
    glu = lambda a, g: a.astype(F32) * jax.nn.sigmoid(g.astype(F32))
    yext_ref[HALO:HALO + ROW_TILE, :] = glu(a_ref[...], g_ref[...])
    yext_ref[0:HALO, :] = jnp.where(has_left, glu(ap_ref[...], gp_ref[...]), 0.0)
    yext_ref[HALO + ROW_TILE:, :] = jnp.where(has_right, glu(an_ref[...], gn_ref[...]), 0.0)
    n_sh = ysh_ref.shape[1]
    for r in range(1, 8):
        ysh_ref[r - 1] = yext_ref[r:r + n_sh, :]
    off = HALO - CONV_PAD
    for c in range(W_GROUP // 128):
        cs = slice(c * 128, (c + 1) * 128)
        acc = jnp.zeros((ROW_TILE, 128), F32)
        for k in range(CONV_WIDTH):
            q8, r = 8 * ((off + k) // 8), (off + k) % 8
            tap = yext_ref[q8:q8 + ROW_TILE, cs] if r == 0 else ysh_ref[r - 1, q8:q8 + ROW_TILE, cs]
            acc = acc + tap * w_ref[k:k + 1, cs]
        cbuf_ref[:, cs] = acc + cb_ref[:, cs]
    y = _layernorm_rows(cbuf_ref[...], lg_ref[...], lb_ref[...])
    o_ref[...] = (_silu(y) * _silu(z_ref[...].astype(F32))).astype(BF16)


def _conv(P, n_rows, n_lat_rows, seq, conv_w, conv_b, ln_g, ln_b):
    hpt = ROW_TILE // HALO
    nhalo = P.shape[0] // HALO
    col = lambda off: pl.BlockSpec((ROW_TILE, W_GROUP), lambda i: (i, off // W_GROUP))
    prev = lambda off: pl.BlockSpec((HALO, W_GROUP), lambda i: (jnp.maximum(i * hpt - 1, 0), off // W_GROUP))
    nxt = lambda off: pl.BlockSpec((HALO, W_GROUP),
                                   lambda i: (jnp.minimum((i + 1) * hpt, nhalo - 1), off // W_GROUP))
    vec = pl.BlockSpec((1, W_GROUP), lambda i: (0, 0))
    return pl.pallas_call(
        functools.partial(_conv_kernel, tps=seq // ROW_TILE, n_lat=n_lat_rows // ROW_TILE),
        grid=(n_rows // ROW_TILE,),
        in_specs=[col(OFF_B_A), col(OFF_B_G), col(OFF_B_Z), prev(OFF_B_A), prev(OFF_B_G), nxt(OFF_B_A),
                  nxt(OFF_B_G), pl.BlockSpec((CONV_WIDTH, W_GROUP), lambda i: (0, 0)), vec, vec, vec],
        out_specs=pl.BlockSpec((ROW_TILE, W_GROUP), lambda i: (i, 0)),
        out_shape=jax.ShapeDtypeStruct((n_rows, W_GROUP), BF16),
        scratch_shapes=[pltpu.VMEM((ROW_TILE + 2 * HALO, W_GROUP), F32),
                        pltpu.VMEM((7, ROW_TILE + 2 * HALO - 8, W_GROUP), F32),
                        pltpu.VMEM((ROW_TILE, W_GROUP), F32)],
        compiler_params=_cparams(("parallel",)),
        name="conv",
    )(P, P, P, P, P, P, P, conv_w, conv_b.reshape(1, W_GROUP), ln_g.reshape(1, W_GROUP),
      ln_b.reshape(1, W_GROUP))


def _rope_tables(seq):
    t = jnp.arange(seq)
    pos = jnp.stack([t // GRID_W, t % GRID_W], axis=-1).astype(F32)
    freq = ROPE_THETA ** (-jnp.arange(ROPE_PAIRS, dtype=F32) / ROPE_PAIRS)
    ang = pos[:, :, None] * freq
    cos, sin = jnp.cos(ang), jnp.sin(ang)
    cfull = jnp.concatenate([cos[:, 0], cos[:, 0], cos[:, 1], cos[:, 1]], axis=-1)
    sfull = jnp.concatenate([-sin[:, 0], sin[:, 0], -sin[:, 1], sin[:, 1]], axis=-1)
    cfull = jnp.concatenate([cfull, jnp.ones((ROW_TILE, HEAD_DIM), F32)], axis=0)
    sfull = jnp.concatenate([sfull, jnp.zeros((ROW_TILE, HEAD_DIM), F32)], axis=0)
    return cfull, sfull


def _norm_rope(x, g, cos, sin, swap_lo):
    y = x * lax.rsqrt(jnp.mean(x * x, axis=-1, keepdims=True) + EPS) * g
    partner = jnp.where(swap_lo, pltpu.roll(y, HEAD_DIM - ROPE_PAIRS, axis=1), pltpu.roll(y, ROPE_PAIRS, axis=1))
    return y * cos + partner * sin


def _qk_kernel(q_ref, k_ref, cos_ref, sin_ref, qg_ref, kg_ref, qo_ref, ko_ref):
    cos, sin = cos_ref[...], sin_ref[...]
    lane = lax.broadcasted_iota(jnp.int32, (ROW_TILE, HEAD_DIM), 1)
    swap_lo = (lane & ROPE_PAIRS) == 0
    q_scale = HEAD_DIM ** -0.5 * math.log2(math.e)
    for hd in range(N_HEADS):
        cs = slice(hd * HEAD_DIM, (hd + 1) * HEAD_DIM)
        r = _norm_rope(q_ref[:, cs].astype(F32), qg_ref[...], cos, sin, swap_lo)
        qo_ref[:, cs] = (r * q_scale).astype(BF16)
    for hd in range(KV_HEADS):
        cs = slice(hd * HEAD_DIM, (hd + 1) * HEAD_DIM)
        ko_ref[:, cs] = _norm_rope(k_ref[:, cs].astype(F32), kg_ref[...], cos, sin, swap_lo).astype(BF16)


def _qk_prep(P, cos, sin, qg, kg, n_lat_rows, seq):
    M = P.shape[0]
    tps = seq // ROW_TILE
    n_lat = n_lat_rows // ROW_TILE
    kvw = KV_HEADS * HEAD_DIM
    tab = pl.BlockSpec((ROW_TILE, HEAD_DIM), lambda i: (jnp.where(i < n_lat, i % tps, tps), 0))
    vec = pl.BlockSpec((1, HEAD_DIM), lambda i: (0, 0))
    return pl.pallas_call(
        _qk_kernel,
        grid=(M // ROW_TILE,),
        in_specs=[pl.BlockSpec((ROW_TILE, W_GROUP), lambda i: (i, OFF_C_Q // W_GROUP)),
                  pl.BlockSpec((ROW_TILE, kvw), lambda i: (i, OFF_C_K // kvw)), tab, tab, vec, vec],
        out_specs=[pl.BlockSpec((ROW_TILE, W_GROUP), lambda i: (i, 0)),
                   pl.BlockSpec((ROW_TILE, kvw), lambda i: (i, 0))],
        out_shape=[jax.ShapeDtypeStruct((M, W_GROUP), BF16), jax.ShapeDtypeStruct((M, kvw), BF16)],
        compiler_params=_cparams(("parallel",)),
        name="qk_prep",
    )(P, P, cos, sin, qg.reshape(1, HEAD_DIM), kg.reshape(1, HEAD_DIM))


def _attn_kernel(*refs, tq, rows, n_sets, kb):
    q_ref, kv_refs, z_ref, o_ref = refs[0], refs[1:1 + 2 * n_sets], refs[1 + 2 * n_sets], refs[2 + 2 * n_sets]
    for j in range(Q_PER_KV):
        cs = slice(j * HEAD_DIM, (j + 1) * HEAD_DIM)
        for r in range(tq // rows):
            rs = slice(r * rows, (r + 1) * rows)
            q = q_ref[rs, cs]
            m = l = acc = None
            for t in range(n_sets):
                k_ref, v_ref = kv_refs[2 * t], kv_refs[2 * t + 1]
                for c0 in range(0, k_ref.shape[0], kb):
                    ks = slice(c0, min(c0 + kb, k_ref.shape[0]))
                    s = lax.dot_general(q, k_ref[ks, :], NT_DIMS, preferred_element_type=F32)
                    bm = jnp.max(s, axis=-1, keepdims=True)
                    m_new = bm if m is None else jnp.maximum(m, bm)
                    p = jnp.exp2(s - m_new)
                    pv = jnp.dot(p.astype(BF16), v_ref[ks, :], preferred_element_type=F32)
                    ps = jnp.sum(p, axis=-1, keepdims=True)
                    if m is None:
                        l, acc = ps, pv
                    else:
                        alpha = jnp.exp2(m - m_new)
                        l, acc = alpha * l + ps, alpha * acc + pv
                    m = m_new
            o_ref[rs, cs] = (acc / l * _silu(z_ref[rs, cs].astype(F32))).astype(BF16)


def _attend(qr, kr, P, B, *, tq, rows, kb, q_block0, n_q, key_sets):
    gw = Q_PER_KV * HEAD_DIM
    qrow = lambda b, g, i: q_block0 + b * (n_q // tq) + i
    in_specs = [pl.BlockSpec((tq, gw), lambda b, g, i: (qrow(b, g, i), g))]
    args = [qr]
    for n_k, k_block0 in key_sets:
        in_specs.append(pl.BlockSpec((n_k, HEAD_DIM), lambda b, g, i, o=k_block0: (o + b, g)))
        in_specs.append(pl.BlockSpec((n_k, HEAD_DIM), lambda b, g, i, o=k_block0: (o + b, OFF_C_V // HEAD_DIM + g)))
        args += [kr, P]
    in_specs.append(pl.BlockSpec((tq, gw), lambda b, g, i: (qrow(b, g, i), OFF_C_Z // gw + g)))
    args.append(P)
    return pl.pallas_call(
        functools.partial(_attn_kernel, tq=tq, rows=rows, n_sets=len(key_sets), kb=kb),
        grid=(B, KV_HEADS, n_q // tq),
        in_specs=in_specs,
        out_specs=pl.BlockSpec((tq, gw), lambda b, g, i: (b * (n_q // tq) + i, g)),
        out_shape=jax.ShapeDtypeStruct((B * n_q, W_GROUP), BF16),
        compiler_params=_cparams(("parallel", "parallel", "parallel"), V7X_VMEM_LIMIT),
        name=f"attn_q{n_q}",
    )(*args)


def _log_sigmoid(x):
    return jnp.minimum(x, 0.0) - jnp.log1p(jnp.exp(-jnp.abs(x)))


def _split3(x):
    hi = x.astype(BF16)
    r = x - hi.astype(F32)
    mid = r.astype(BF16)
    lo = (r - mid.astype(F32)).astype(BF16)
    return hi, mid, lo


def _mlstm_dir(direction, q_refs, k_refs, v_refs, g_ref, gt_ref, brow_ref, bcol_ref, sel_ref, o_ref, cext_ref, m_ref):
    rev = direction == 1
    end = 0 if rev else CHUNK - 1
    scale = HEAD_DIM ** -0.5
    row = lax.broadcasted_iota(jnp.int32, (CHUNK, CHUNK), 0)
    colm = lax.broadcasted_iota(jnp.int32, (CHUNK, CHUNK), 1)
    allowed = (colm >= row) if rev else (colm <= row)
    tri = jnp.where(allowed, 1.0, 0.0).astype(BF16)
    eye = jnp.where(colm == row, 1.0, 0.0).astype(BF16)

    pre_t = gt_ref[...] + bcol_ref[...]
    lf = _log_sigmoid(g_ref[...] + brow_ref[...])
    b = sum(jnp.dot(tri, part, preferred_element_type=F32) for part in _split3(lf))
    b_t = sum(lax.dot_general(part, tri, NT_DIMS, preferred_element_type=F32)
              for part in _split3(_log_sigmoid(pre_t)))
    bc_all = jnp.dot(jnp.concatenate(_split3(b), axis=1), sel_ref[direction], preferred_element_type=F32)

    hph = N_HEADS // 2
    for hd in range(N_HEADS):
        cs = slice((hd % hph) * HEAD_DIM, (hd % hph + 1) * HEAD_DIM)
        hs = slice(hd * HEAD_DIM, (hd + 1) * HEAD_DIM)
        st = direction * N_HEADS + hd
        ic = direction * 2 * N_HEADS + hd
        fc = ic + N_HEADS
        a_row = pre_t[ic:ic + 1, :] - b_t[fc:fc + 1, :]
        a_mask = jnp.where(allowed, a_row, -jnp.inf)
        m_prev = m_ref[st]
        g = jnp.maximum(m_prev, jnp.max(a_mask, axis=-1, keepdims=True))
        w_intra = jnp.exp(a_mask - g)
        w_inter = jnp.exp(m_prev - g) * scale
        bc = bc_all[:, hs]
        m_row = bc + g
        qh, kh, vh = q_refs[hd // hph][:, cs], k_refs[hd // hph][:, cs], v_refs[hd // hph][:, cs]
        kc = jnp.concatenate([kh, cext_ref[st].astype(BF16)], axis=0)
        qkc = lax.dot_general(qh, kc, NT_DIMS, preferred_element_type=F32)
        s = qkc[:, 0:CHUNK] * (w_intra * scale)
        num = jnp.dot(s.astype(BF16), vh, preferred_element_type=F32) + w_inter * qkc[:, CHUNK:CHUNK + HEAD_DIM]
        den = jnp.sum(s, axis=-1, keepdims=True) + w_inter * qkc[:, CHUNK + HEAD_DIM:]
        o_ref[:, hs] = num / jnp.maximum(jnp.abs(den), jnp.exp(-m_row))

        m_end, b_end = m_row[end:end + 1, :], bc[end:end + 1, :]
        w_end = jnp.exp(a_row + (b_end - m_end))
        decay = jnp.exp(b_end + m_prev - m_end)
        v_t = lax.dot_general(eye, vh, NT_DIMS, preferred_element_type=F32)
        lhs = jnp.concatenate([(v_t * w_end).astype(BF16),
                               jnp.broadcast_to(w_end, (HEAD_DIM, CHUNK)).astype(BF16)], axis=0)
        cext_ref[st] = decay * cext_ref[st] + jnp.dot(lhs, kh, preferred_element_type=F32)
        m_ref[st] = m_end


def _mlstm_kernel(*refs):
    brow_ref, bcol_ref, sel_ref = refs[16:19]
    of_ref, ob_ref, cext_ref, m_ref = refs[19:23]

    @pl.when(pl.program_id(1) == 0)
    def _():
        cext_ref[...] = jnp.zeros_like(cext_ref)
        m_ref[...] = jnp.zeros_like(m_ref)

    for d, o_ref in ((0, of_ref), (1, ob_ref)):
        r = refs[8 * d:8 * d + 8]
        _mlstm_dir(d, r[0:2], r[2:4], r[4:6], r[6], r[7], brow_ref, bcol_ref, sel_ref, o_ref, cext_ref, m_ref)


def _forget_selector():
    sel = np.zeros((N_DIR, 3 * GATE_PAD, W_GROUP), np.float32)
    for d in range(N_DIR):
        for hd in range(N_HEADS):
            fc = d * 2 * N_HEADS + N_HEADS + hd
            for part in range(3):
                sel[d, part * GATE_PAD + fc, hd * HEAD_DIM:(hd + 1) * HEAD_DIM] = 1.0
    return jnp.asarray(sel, BF16)


def _mlstm(P, gates, gates_t, bias_row, bias_col, B, seq, ctx_len):
    M = P.shape[0]
    ncl, ncc = seq // CHUNK, ctx_len // CHUNK
    nc = ncl + ncc
    ctx0 = B * ncl
    fwd = lambda b, j: jnp.where(j < ncc, ctx0 + b * ncc + j, b * ncl + j - ncc)
    bwd = lambda b, j: jnp.where(j < ncc, ctx0 + b * ncc + ncc - 1 - j, b * ncl + nc - 1 - j)
    in_specs, args = [], []
    for chunk in (fwd, bwd):
        for off in (OFF_D_Q, OFF_D_K, OFF_D_V):
            for half in range(2):
                in_specs.append(pl.BlockSpec((CHUNK, HALF), lambda b, j, c=chunk, o=off // HALF + half: (c(b, j), o)))
                args.append(P)
        in_specs.append(pl.BlockSpec((CHUNK, GATE_PAD), lambda b, j, c=chunk: (c(b, j), 0)))
        in_specs.append(pl.BlockSpec((GATE_PAD, CHUNK), lambda b, j, c=chunk: (0, c(b, j))))
        args += [gates, gates_t]
    in_specs += [pl.BlockSpec((1, GATE_PAD), lambda b, j: (0, 0)), pl.BlockSpec((GATE_PAD, 1), lambda b, j: (0, 0)),
                 pl.BlockSpec((N_DIR, 3 * GATE_PAD, W_GROUP), lambda b, j: (0, 0, 0))]
    args += [bias_row, bias_col, _forget_selector()]
    return pl.pallas_call(
        _mlstm_kernel,
        grid=(B, nc),
        in_specs=in_specs,
        out_specs=[pl.BlockSpec((CHUNK, W_GROUP), lambda b, j: (fwd(b, j), 0)),
                   pl.BlockSpec((CHUNK, W_GROUP), lambda b, j: (bwd(b, j), 0))],
        out_shape=[jax.ShapeDtypeStruct((M, W_GROUP), F32), jax.ShapeDtypeStruct((M, W_GROUP), F32)],
        scratch_shapes=[pltpu.VMEM((N_DIR * N_HEADS, 2 * HEAD_DIM, HEAD_DIM), F32),
                        pltpu.VMEM((N_DIR * N_HEADS, 1, HEAD_DIM), F32)],
        compiler_params=_cparams(("parallel", "arbitrary")),
        name="mlstm",
    )(*args)


def _memout_kernel(hf_ref, hb_ref, o_ref, z_ref, g_ref, out_ref):
    for hd in range(HALF // HEAD_DIM):
        cs = slice(hd * HEAD_DIM, (hd + 1) * HEAD_DIM)
        h = hf_ref[:, cs] + hb_ref[:, cs]
        hn = h * lax.rsqrt(jnp.mean(h * h, axis=-1, keepdims=True) + EPS) * g_ref[:, cs]
        out_ref[:, cs] = (hn * jax.nn.sigmoid(o_ref[:, cs].astype(F32)) * _silu(z_ref[:, cs].astype(F32))).astype(BF16)


def _mem_out(hf, hb, P, n_rows, g, rows=512):
    blk = lambda off: pl.BlockSpec((rows, HALF), lambda i, hh: (i, off // HALF + hh))
    return pl.pallas_call(
        _memout_kernel,
        grid=(n_rows // rows, 2),
        in_specs=[blk(0), blk(0), blk(OFF_D_O), blk(OFF_D_Z), pl.BlockSpec((1, HALF), lambda i, hh: (0, hh))],
        out_specs=blk(0),
        out_shape=jax.ShapeDtypeStruct((n_rows, W_GROUP), BF16),
        compiler_params=_cparams(("parallel", "parallel")),
        name="mem_out",
    )(hf, hb, P, P, g.reshape(1, W_GROUP))


def _out_kernel(a_ref, b_ref, c_ref, d_ref, w_ref, res_ref, gate_ref, o_ref):
    x = jnp.concatenate([a_ref[...], b_ref[...], c_ref[...], d_ref[...]], axis=1)
    o_ref[...] = res_ref[...] + gate_ref[...] * jnp.dot(x, w_ref[...], preferred_element_type=F32)


def _out_proj(a, b, att, mem, w_all, layer, res, mod, *, row_block0, gate_row, tm, tn):
    n_rows, D = res.shape
    gcol = 2 * D // tn
    shifted = pl.BlockSpec((tm, W_GROUP), lambda n, m: (row_block0 + m, 0))
    return pl.pallas_call(
        _out_kernel,
        grid=(D // tn, n_rows // tm),
        in_specs=[shifted, shifted, pl.BlockSpec((tm, W_GROUP), lambda n, m: (m, 0)), shifted,
                  pl.BlockSpec((None, 4 * W_GROUP, tn), lambda n, m: (layer, 0, n)),
                  pl.BlockSpec((tm, tn), lambda n, m: (m, n)),
                  pl.BlockSpec((None, 1, tn), lambda n, m: (gate_row(m), 0, gcol + n))],
        out_specs=pl.BlockSpec((tm, tn), lambda n, m: (m, n)),
        out_shape=jax.ShapeDtypeStruct((n_rows, D), F32),
        compiler_params=_cparams(("parallel", "parallel"), V7X_VMEM_LIMIT),
        name="out_proj",
    )(a, b, att, mem, w_all, res, mod)


def kernel(x, c, ctx, c_ctx, w_ada, b_ada, norm_g, w_in, sgu_w, sgu_b, sgu_ln_g, sgu_ln_b, conv_w, conv_b,
           conv_ln_g, conv_ln_b, q_norm_g, k_norm_g, mlstm_i_bias, mlstm_f_bias, mh_norm_g, w_out):
    B, seq, D = x.shape
    ctx_len = ctx.shape[1]
    depth = w_ada.shape[0]
    ML, MC = B * seq, B * ctx_len
    M = ML + MC
    assert ctx_len == ROW_TILE and seq % 512 == 0 and M % 512 == 0 and D == 4 * W_GROUP and B < 8
    assert w_in.shape[2] == P_MAIN + GATE_COLS

    h_lat, h_ctx = x.reshape(ML, D), ctx.reshape(MC, D)
    cc = jnp.zeros((8, D), F32).at[:B].set(c).at[B].set(c_ctx)
    mod_all = _ada_mod(cc, w_ada, b_ada)
    cos, sin = _rope_tables(seq)
    w_out_b, sgu_w_b = w_out.astype(BF16), sgu_w.astype(BF16)
    w_in_t = jnp.swapaxes(w_in, 1, 2)
    tm_in = M // 8

    for l in range(depth):
        last = l == depth - 1
        n_rows = ML if last else M
        wgt = jnp.pad(w_in_t[l, P_MAIN:, :], ((0, GATE_PAD - GATE_COLS), (0, 0)))
        mod = mod_all[l].reshape(8, 1, 3 * D)
        hn, gates, gates_t = _norm_mod(h_lat, h_ctx, mod, norm_g[l], wgt.T, wgt, B, seq)
        P = _in_proj(hn, w_in_t, l, tm=tm_in, tn=512)

        a_out = _gmlp(P, n_rows, sgu_w_b[l], sgu_b[l].reshape(N_HEADS, CHUNK, 1), sgu_ln_g[l], sgu_ln_b[l])
        b_out = _conv(P, n_rows, ML, seq, conv_w[l], conv_b[l], conv_ln_g[l], conv_ln_b[l])

        qr, kr = _qk_prep(P, cos, sin, q_norm_g[l], k_norm_g[l], ML, seq)
        att_lat = _attend(qr, kr, P, B, tq=512, rows=512, kb=seq, q_block0=0, n_q=seq,
                          key_sets=((seq, 0), (ctx_len, ML // ctx_len)))

        bias = jnp.concatenate([mlstm_i_bias[l], mlstm_f_bias[l]], axis=-1).reshape(GATE_COLS)
        bias = jnp.pad(bias, (0, GATE_PAD - GATE_COLS))
        hf, hb = _mlstm(P, gates, gates_t, bias.reshape(1, GATE_PAD), bias.reshape(GATE_PAD, 1), B, seq, ctx_len)
        mem = _mem_out(hf, hb, P, n_rows, mh_norm_g[l])

        if not last:
            att_ctx = _attend(qr, kr, P, B, tq=ctx_len, rows=128, kb=ctx_len, q_block0=ML // ctx_len, n_q=ctx_len,
                              key_sets=((ctx_len, ML // ctx_len),))
            h_ctx = _out_proj(a_out, b_out, att_ctx, mem, w_out_b, l, h_ctx, mod, row_block0=ML // MC,
                              gate_row=lambda m: B, tm=MC, tn=1024)
        h_lat = _out_proj(a_out, b_out, att_lat, mem, w_out_b, l, h_lat, mod, row_block0=0,
                          gate_row=lambda m: (m * 512) // seq, tm=512, tn=1024)

    return h_lat.reshape(B, seq, D)
```

```python
import functools
import math

import jax
import jax.numpy as jnp
import numpy as np
from jax import lax
from jax.experimental import pallas as pl
from jax.experimental.pallas import tpu as pltpu

F32 = jnp.float32
BF16 = jnp.bfloat16

HEAD_DIM = 128
N_HEADS = 8
KV_HEADS = 2
Q_PER_KV = N_HEADS // KV_HEADS
W_GROUP = N_HEADS * HEAD_DIM
HALF = W_GROUP // 2
CHUNK = 128
CONV_WIDTH = 31
CONV_PAD = CONV_WIDTH // 2
GRID_W = 64
ROPE_THETA = 10000.0
ROPE_PAIRS = HEAD_DIM // 4
N_DIR = 2
EPS = 1e-6
GATE_COLS = N_DIR * 2 * N_HEADS
GATE_PAD = 128

OFF_A_U, OFF_A_V, OFF_A_Z = 0, W_GROUP, 2 * W_GROUP
OFF_B_A, OFF_B_G, OFF_B_Z = 3 * W_GROUP, 4 * W_GROUP, 5 * W_GROUP
OFF_C_Q = 6 * W_GROUP
OFF_C_K = 7 * W_GROUP
OFF_C_V = OFF_C_K + KV_HEADS * HEAD_DIM
OFF_C_Z = OFF_C_V + KV_HEADS * HEAD_DIM
OFF_D_Q = OFF_C_Z + W_GROUP
OFF_D_K, OFF_D_V, OFF_D_O, OFF_D_Z = (OFF_D_Q + i * W_GROUP for i in range(1, 5))
P_MAIN = OFF_D_Z + W_GROUP

ROW_TILE = 256
HALO = 16
V7X_VMEM_LIMIT = 56 * 1024 * 1024
V7X_VMEM_BYTES = 64 * 1024 * 1024
NT_DIMS = (((1,), (1,)), ((), ()))


def _silu(x):
    return x * jax.nn.sigmoid(x)


def _cparams(sem, vmem=None):
    return pltpu.CompilerParams(dimension_semantics=sem, vmem_limit_bytes=vmem)


def _ada_kernel(c_ref, w_ref, b_ref, o_ref):
    act = _silu(c_ref[...])
    o_ref[...] = jnp.dot(act.astype(BF16), w_ref[...].astype(BF16), preferred_element_type=F32) + b_ref[...]


def _ada_mod(cc, w_ada, b_ada, tn=512):
    L, D, N = w_ada.shape
    return pl.pallas_call(
        _ada_kernel,
        grid=(L, N // tn),
        in_specs=[pl.BlockSpec((8, D), lambda l, n: (0, 0)),
                  pl.BlockSpec((None, D, tn), lambda l, n: (l, 0, n)),
                  pl.BlockSpec((None, 1, tn), lambda l, n: (l, 0, n))],
        out_specs=pl.BlockSpec((None, 8, tn), lambda l, n: (l, 0, n)),
        out_shape=jax.ShapeDtypeStruct((L, 8, N), F32),
        compiler_params=_cparams(("parallel", "parallel")),
        name="ada_mod",
    )(cc, w_ada, b_ada.reshape(L, 1, N))


def _norm_kernel(x_ref, c_ref, mod_ref, g_ref, wg_ref, wgt_ref, hn_ref, gates_ref, gatest_ref, *, D, n_lat):
    x = jnp.where(pl.program_id(0) >= n_lat, c_ref[...], x_ref[...])
    y = x * lax.rsqrt(jnp.mean(x * x, axis=-1, keepdims=True) + EPS) * g_ref[...]
    hb = (y * (1.0 + mod_ref[:, D:2 * D]) + mod_ref[:, 0:D]).astype(BF16)
    hn_ref[...] = hb
    gates_ref[...] = jnp.dot(hb, wg_ref[...].astype(BF16), preferred_element_type=F32)
    gatest_ref[...] = lax.dot_general(wgt_ref[...].astype(BF16), hb, NT_DIMS, preferred_element_type=F32)


def _norm_mod(h_lat, h_ctx, mod, norm_g, wg, wgt, B, seq):
    D = h_lat.shape[1]
    n_lat = h_lat.shape[0] // ROW_TILE
    n_tiles = n_lat + h_ctx.shape[0] // ROW_TILE
    M = n_tiles * ROW_TILE
    tps = seq // ROW_TILE
    return pl.pallas_call(
        functools.partial(_norm_kernel, D=D, n_lat=n_lat),
        grid=(n_tiles,),
        in_specs=[pl.BlockSpec((ROW_TILE, D), lambda i: (jnp.minimum(i, n_lat - 1), 0)),
                  pl.BlockSpec((ROW_TILE, D), lambda i: (jnp.maximum(i - n_lat, 0), 0)),
                  pl.BlockSpec((None, 1, 3 * D), lambda i: (jnp.where(i < n_lat, i // tps, B), 0, 0)),
                  pl.BlockSpec((1, D), lambda i: (0, 0)),
                  pl.BlockSpec((D, GATE_PAD), lambda i: (0, 0)),
                  pl.BlockSpec((GATE_PAD, D), lambda i: (0, 0))],
        out_specs=[pl.BlockSpec((ROW_TILE, D), lambda i: (i, 0)),
                   pl.BlockSpec((ROW_TILE, GATE_PAD), lambda i: (i, 0)),
                   pl.BlockSpec((GATE_PAD, ROW_TILE), lambda i: (0, i))],
        out_shape=[jax.ShapeDtypeStruct((M, D), BF16),
                   jax.ShapeDtypeStruct((M, GATE_PAD), F32),
                   jax.ShapeDtypeStruct((GATE_PAD, M), F32)],
        compiler_params=_cparams(("parallel",)),
        name="norm_mod",
    )(h_lat, h_ctx, mod, norm_g.reshape(1, D), wg, wgt)


def _mm_kernel(x_ref, w_ref, o_ref, wb_ref):
    @pl.when(pl.program_id(1) == 0)
    def _():
        wb_ref[...] = w_ref[...].astype(BF16)

    o_ref[...] = lax.dot_general(x_ref[...], wb_ref[...], NT_DIMS, preferred_element_type=F32).astype(o_ref.dtype)


def _in_proj(hn, w_t, layer, tm, tn):
    M, K = hn.shape
    return pl.pallas_call(
        _mm_kernel,
        grid=(P_MAIN // tn, M // tm),
        in_specs=[pl.BlockSpec((tm, K), lambda n, m: (m, 0)),
                  pl.BlockSpec((None, tn, K), lambda n, m: (layer, n, 0))],
        out_specs=pl.BlockSpec((tm, tn), lambda n, m: (m, n)),
        out_shape=jax.ShapeDtypeStruct((M, P_MAIN), BF16),
        scratch_shapes=[pltpu.VMEM((tn, K), BF16)],
        compiler_params=_cparams(("parallel", "arbitrary"), V7X_VMEM_BYTES - 3 * 1024 * 1024),
        name="in_proj",
    )(hn, w_t)


def _layernorm_rows(x, g, b):
    mu = jnp.mean(x, axis=-1, keepdims=True)
    xc = x - mu
    return xc * lax.rsqrt(jnp.mean(xc * xc, axis=-1, keepdims=True) + EPS) * g + b


def _gmlp_kernel(u_ref, v_ref, z_ref, ws_ref, bs_ref, lg_ref, lb_ref, o_ref, *, rows):
    for c in range(rows // CHUNK):
        rs = slice(c * CHUNK, (c + 1) * CHUNK)
        vn = _layernorm_rows(v_ref[rs, :].astype(F32), lg_ref[...], lb_ref[...]).astype(BF16)
        for hd in range(N_HEADS):
            cs = slice(hd * HEAD_DIM, (hd + 1) * HEAD_DIM)
            mixed = jnp.dot(ws_ref[hd], vn[:, cs], preferred_element_type=F32) + bs_ref[hd]
            u = u_ref[rs, cs].astype(F32)
            o_ref[rs, cs] = (u * mixed * _silu(z_ref[rs, cs].astype(F32))).astype(BF16)


def _gmlp(P, n_rows, ws, bs, ln_g, ln_b, rows=512):
    col = lambda off: pl.BlockSpec((rows, W_GROUP), lambda i: (i, off // W_GROUP))
    return pl.pallas_call(
        functools.partial(_gmlp_kernel, rows=rows),
        grid=(n_rows // rows,),
        in_specs=[col(OFF_A_U), col(OFF_A_V), col(OFF_A_Z),
                  pl.BlockSpec((N_HEADS, CHUNK, CHUNK), lambda i: (0, 0, 0)),
                  pl.BlockSpec((N_HEADS, CHUNK, 1), lambda i: (0, 0, 0)),
                  pl.BlockSpec((1, W_GROUP), lambda i: (0, 0)),
                  pl.BlockSpec((1, W_GROUP), lambda i: (0, 0))],
        out_specs=pl.BlockSpec((rows, W_GROUP), lambda i: (i, 0)),
        out_shape=jax.ShapeDtypeStruct((n_rows, W_GROUP), BF16),
        compiler_params=_cparams(("parallel",)),
        name="gmlp",
    )(P, P, P, ws, bs, ln_g.reshape(1, W_GROUP), ln_b.reshape(1, W_GROUP))


def _conv_kernel(a_ref, g_ref, z_ref, ap_ref, gp_ref, an_ref, gn_ref, w_ref, cb_ref, lg_ref, lb_ref, o_ref,
                 yext_ref, ysh_ref, cbuf_ref, *, tps, n_lat):
    i = pl.program_id(0)
    is_lat = i < n_lat
    has_left = jnp.logical_and(is_lat, i % tps != 0)
    has_right = jnp.logical_and(is_lat, i % tps != tps - 1)
    glu = lambda a, g: a.astype(F32) * jax.nn.sigmoid(g.astype(F32))
    yext_ref[HALO:HALO + ROW_TILE, :] = glu(a_ref[...], g_ref[...])
    yext_ref[0:HALO, :] = jnp.where(has_left, glu(ap_ref[...], gp_ref[...]), 0.0)
    yext_ref[HALO + ROW_TILE:, :] = jnp.where(has_right, glu(an_ref[...], gn_ref[...]), 0.0)
    n_sh = ysh_ref.shape[1]
    for r in range(1, 8):
        ysh_ref[r - 1] = yext_ref[r:r + n_sh, :]
    off = HALO - CONV_PAD
    for c in range(W_GROUP // 128):
        cs = slice(c * 128, (c + 1) * 128)
        acc = jnp.zeros((ROW_TILE, 128), F32)
        for k in range(CONV_WIDTH):
            q8, r = 8 * ((off + k) // 8), (off + k) % 8
            tap = yext_ref[q8:q8 + ROW_TILE, cs] if r == 0 else ysh_ref[r - 1, q8:q8 + ROW_TILE, cs]
            acc = acc + tap * w_ref[k:k + 1, cs]
        cbuf_ref[:, cs] = acc + cb_ref[:, cs]
    y = _layernorm_rows(cbuf_ref[...], lg_ref[...], lb_ref[...])
    o_ref[...] = (_silu(y) * _silu(z_ref[...].astype(F32))).astype(BF16)


def _conv(P, n_rows, n_lat_rows, seq, conv_w, conv_b, ln_g, ln_b):
    hpt = ROW_TILE // HALO
    nhalo = P.shape[0] // HALO
    col = lambda off: pl.BlockSpec((ROW_TILE, W_GROUP), lambda i: (i, off // W_GROUP))
    prev = lambda off: pl.BlockSpec((HALO, W_GROUP), lambda i: (jnp.maximum(i * hpt - 1, 0), off // W_GROUP))
    nxt = lambda off: pl.BlockSpec((HALO, W_GROUP),
                                   lambda i: (jnp.minimum((i + 1) * hpt, nhalo - 1), off // W_GROUP))
    vec = pl.BlockSpec((1, W_GROUP), lambda i: (0, 0))
    return pl.pallas_call(
        functools.partial(_conv_kernel, tps=seq // ROW_TILE, n_lat=n_lat_rows // ROW_TILE),
        grid=(n_rows // ROW_TILE,),
        in_specs=[col(OFF_B_A), col(OFF_B_G), col(OFF_B_Z), prev(OFF_B_A), prev(OFF_B_G), nxt(OFF_B_A),
                  nxt(OFF_B_G), pl.BlockSpec((CONV_WIDTH, W_GROUP), lambda i: (0, 0)), vec, vec, vec],
        out_specs=pl.BlockSpec((ROW_TILE, W_GROUP), lambda i: (i, 0)),
        out_shape=jax.ShapeDtypeStruct((n_rows, W_GROUP), BF16),
        scratch_shapes=[pltpu.VMEM((ROW_TILE + 2 * HALO, W_GROUP), F32),
                        pltpu.VMEM((7, ROW_TILE + 2 * HALO - 8, W_GROUP), F32),
                        pltpu.VMEM((ROW_TILE, W_GROUP), F32)],
        compiler_params=_cparams(("parallel",)),
        name="conv",
    )(P, P, P, P, P, P, P, conv_w, conv_b.reshape(1, W_GROUP), ln_g.reshape(1, W_GROUP),
      ln_b.reshape(1, W_GROUP))


def _rope_tables(seq):
    t = jnp.arange(seq)
    pos = jnp.stack([t // GRID_W, t % GRID_W], axis=-1).astype(F32)
    freq = ROPE_THETA ** (-jnp.arange(ROPE_PAIRS, dtype=F32) / ROPE_PAIRS)
    ang = pos[:, :, None] * freq
    cos, sin = jnp.cos(ang), jnp.sin(ang)
    cfull = jnp.concatenate([cos[:, 0], cos[:, 0], cos[:, 1], cos[:, 1]], axis=-1)
    sfull = jnp.concatenate([-sin[:, 0], sin[:, 0], -sin[:, 1], sin[:, 1]], axis=-1)
    cfull = jnp.concatenate([cfull, jnp.ones((ROW_TILE, HEAD_DIM), F32)], axis=0)
    sfull = jnp.concatenate([sfull, jnp.zeros((ROW_TILE, HEAD_DIM), F32)], axis=0)
    return cfull, sfull


def _norm_rope(x, g, cos, sin, swap_lo):
    y = x * lax.rsqrt(jnp.mean(x * x, axis=-1, keepdims=True) + EPS) * g
    partner = jnp.where(swap_lo, pltpu.roll(y, HEAD_DIM - ROPE_PAIRS, axis=1), pltpu.roll(y, ROPE_PAIRS, axis=1))
    return y * cos + partner * sin


def _qk_kernel(q_ref, k_ref, cos_ref, sin_ref, qg_ref, kg_ref, qo_ref, ko_ref):
    cos, sin = cos_ref[...], sin_ref[...]
    lane = lax.broadcasted_iota(jnp.int32, (ROW_TILE, HEAD_DIM), 1)
    swap_lo = (lane & ROPE_PAIRS) == 0
    q_scale = HEAD_DIM ** -0.5 * math.log2(math.e)
    for hd in range(N_HEADS):
        cs = slice(hd * HEAD_DIM, (hd + 1) * HEAD_DIM)
        r = _norm_rope(q_ref[:, cs].astype(F32), qg_ref[...], cos, sin, swap_lo)
        qo_ref[:, cs] = (r * q_scale).astype(BF16)
    for hd in range(KV_HEADS):
        cs = slice(hd * HEAD_DIM, (hd + 1) * HEAD_DIM)
        ko_ref[:, cs] = _norm_rope(k_ref[:, cs].astype(F32), kg_ref[...], cos, sin, swap_lo).astype(BF16)


def _qk_prep(P, cos, sin, qg, kg, n_lat_rows, seq):
    M = P.shape[0]
    tps = seq // ROW_TILE
    n_lat = n_lat_rows // ROW_TILE
    kvw = KV_HEADS * HEAD_DIM
    tab = pl.BlockSpec((ROW_TILE, HEAD_DIM), lambda i: (jnp.where(i < n_lat, i % tps, tps), 0))
    vec = pl.BlockSpec((1, HEAD_DIM), lambda i: (0, 0))
    return pl.pallas_call(
        _qk_kernel,
        grid=(M // ROW_TILE,),
        in_specs=[pl.BlockSpec((ROW_TILE, W_GROUP), lambda i: (i, OFF_C_Q // W_GROUP)),
                  pl.BlockSpec((ROW_TILE, kvw), lambda i: (i, OFF_C_K // kvw)), tab, tab, vec, vec],
        out_specs=[pl.BlockSpec((ROW_TILE, W_GROUP), lambda i: (i, 0)),
                   pl.BlockSpec((ROW_TILE, kvw), lambda i: (i, 0))],
        out_shape=[jax.ShapeDtypeStruct((M, W_GROUP), BF16), jax.ShapeDtypeStruct((M, kvw), BF16)],
        compiler_params=_cparams(("parallel",)),
        name="qk_prep",
    )(P, P, cos, sin, qg.reshape(1, HEAD_DIM), kg.reshape(1, HEAD_DIM))


def _attn_kernel(*refs, tq, rows, n_sets, kb):
    q_ref, kv_refs, z_ref, o_ref = refs[0], refs[1:1 + 2 * n_sets], refs[1 + 2 * n_sets], refs[2 + 2 * n_sets]
    v_ext = [jnp.concatenate([kv_refs[2 * t + 1][...], jnp.ones(kv_refs[2 * t + 1].shape, BF16)], axis=1)
             for t in range(n_sets)]
    for j in range(Q_PER_KV):
        cs = slice(j * HEAD_DIM, (j + 1) * HEAD_DIM)
        for r in range(tq // rows):
            rs = slice(r * rows, (r + 1) * rows)
            q = q_ref[rs, cs]
            m = acc = None
            for t in range(n_sets):
                k_ref = kv_refs[2 * t]
                for c0 in range(0, k_ref.shape[0], kb):
                    ks = slice(c0, min(c0 + kb, k_ref.shape[0]))
                    s = lax.dot_general(q, k_ref[ks, :], NT_DIMS, preferred_element_type=F32)
                    bm = jnp.max(s, axis=-1, keepdims=True)
                    m_new = bm if m is None else jnp.maximum(m, bm)
                    p = jnp.exp2(s - m_new)
                    pv = jnp.dot(p.astype(BF16), v_ext[t][ks, :], preferred_element_type=F32)
                    acc = pv if m is None else jnp.exp2(m - m_new) * acc + pv
                    m = m_new
            o = acc[:, 0:HEAD_DIM] / acc[:, HEAD_DIM:]
            o_ref[rs, cs] = (o * _silu(z_ref[rs, cs].astype(F32))).astype(BF16)


def _attend(qr, kr, P, B, *, tq, rows, kb, q_block0, n_q, key_sets):
    gw = Q_PER_KV * HEAD_DIM
    qrow = lambda b, g, i: q_block0 + b * (n_q // tq) + i
    in_specs = [pl.BlockSpec((tq, gw), lambda b, g, i: (qrow(b, g, i), g))]
    args = [qr]
    for n_k, k_block0 in key_sets:
        in_specs.append(pl.BlockSpec((n_k, HEAD_DIM), lambda b, g, i, o=k_block0: (o + b, g)))
        in_specs.append(pl.BlockSpec((n_k, HEAD_DIM), lambda b, g, i, o=k_block0: (o + b, OFF_C_V // HEAD_DIM + g)))
        args += [kr, P]
    in_specs.append(pl.BlockSpec((tq, gw), lambda b, g, i: (qrow(b, g, i), OFF_C_Z // gw + g)))
    args.append(P)
    return pl.pallas_call(
        functools.partial(_attn_kernel, tq=tq, rows=rows, n_sets=len(key_sets), kb=kb),
        grid=(B, KV_HEADS, n_q // tq),
        in_specs=in_specs,
        out_specs=pl.BlockSpec((tq, gw), lambda b, g, i: (b * (n_q // tq) + i, g)),
        out_shape=jax.ShapeDtypeStruct((B * n_q, W_GROUP), BF16),
        compiler_params=_cparams(("parallel", "parallel", "parallel"), V7X_VMEM_LIMIT),
        name=f"attn_q{n_q}",
    )(*args)


def _log_sigmoid(x):
    return jnp.minimum(x, 0.0) - jnp.log1p(jnp.exp(-jnp.abs(x)))


def _split3(x):
    hi = x.astype(BF16)
    r = x - hi.astype(F32)
    mid = r.astype(BF16)
    lo = (r - mid.astype(F32)).astype(BF16)
    return hi, mid, lo


def _mlstm_dir(direction, q_refs, k_refs, v_refs, g_ref, gt_ref, brow_ref, bcol_ref, sel_ref, o_ref, cext_ref, m_ref):
    rev = direction == 1
    end = 0 if rev else CHUNK - 1
    scale = HEAD_DIM ** -0.5
    row = lax.broadcasted_iota(jnp.int32, (CHUNK, CHUNK), 0)
    colm = lax.broadcasted_iota(jnp.int32, (CHUNK, CHUNK), 1)
    allowed = (colm >= row) if rev else (colm <= row)
    tri = jnp.where(allowed, 1.0, 0.0).astype(BF16)
    eye = jnp.where(colm == row, 1.0, 0.0).astype(BF16)

    pre_t = gt_ref[...] + bcol_ref[...]
    lf = _log_sigmoid(g_ref[...] + brow_ref[...])
    b = sum(jnp.dot(tri, part, preferred_element_type=F32) for part in _split3(lf))
    b_t = sum(lax.dot_general(part, tri, NT_DIMS, preferred_element_type=F32)
              for part in _split3(_log_sigmoid(pre_t)))
    bc_all = jnp.dot(jnp.concatenate(_split3(b), axis=1), sel_ref[direction], preferred_element_type=F32)

    hph = N_HEADS // 2
    for hd in range(N_HEADS):
        cs = slice((hd % hph) * HEAD_DIM, (hd % hph + 1) * HEAD_DIM)
        hs = slice(hd * HEAD_DIM, (hd + 1) * HEAD_DIM)
        st = direction * N_HEADS + hd
        ic = direction * 2 * N_HEADS + hd
        fc = ic + N_HEADS
        a_row = pre_t[ic:ic + 1, :] - b_t[fc:fc + 1, :]
        a_mask = jnp.where(allowed, a_row, -jnp.inf)
        m_prev = m_ref[st]
        g = jnp.maximum(m_prev, jnp.max(a_mask, axis=-1, keepdims=True))
        w_intra = jnp.exp(a_mask - g)
        w_inter = jnp.exp(m_prev - g) * scale
        bc = bc_all[:, hs]
        m_row = bc + g
        qh, kh, vh = q_refs[hd // hph][:, cs], k_refs[hd // hph][:, cs], v_refs[hd // hph][:, cs]
        kc = jnp.concatenate([kh, cext_ref[st].astype(BF16)], axis=0)
        qkc = lax.dot_general(qh, kc, NT_DIMS, preferred_element_type=F32)
        s = qkc[:, 0:CHUNK] * (w_intra * scale)
        num = jnp.dot(s.astype(BF16), vh, preferred_element_type=F32) + w_inter * qkc[:, CHUNK:CHUNK + HEAD_DIM]
        den = jnp.sum(s, axis=-1, keepdims=True) + w_inter * qkc[:, CHUNK + HEAD_DIM:]
        o_ref[:, hs] = num / jnp.maximum(jnp.abs(den), jnp.exp(-m_row))

        m_end, b_end = m_row[end:end + 1, :], bc[end:end + 1, :]
        w_end = jnp.exp(a_row + (b_end - m_end))
        decay = jnp.exp(b_end + m_prev - m_end)
        v_t = lax.dot_general(eye, vh, NT_DIMS, preferred_element_type=F32)
        lhs = jnp.concatenate([(v_t * w_end).astype(BF16),
                               jnp.broadcast_to(w_end, (HEAD_DIM, CHUNK)).astype(BF16)], axis=0)
        cext_ref[st] = decay * cext_ref[st] + jnp.dot(lhs, kh, preferred_element_type=F32)
        m_ref[st] = m_end


def _mlstm_kernel(*refs):
    brow_ref, bcol_ref, sel_ref = refs[16:19]
    of_ref, ob_ref, cext_ref, m_ref = refs[19:23]

    @pl.when(pl.program_id(1) == 0)
    def _():
        cext_ref[...] = jnp.zeros_like(cext_ref)
        m_ref[...] = jnp.zeros_like(m_ref)

    for d, o_ref in ((0, of_ref), (1, ob_ref)):
        r = refs[8 * d:8 * d + 8]
        _mlstm_dir(d, r[0:2], r[2:4], r[4:6], r[6], r[7], brow_ref, bcol_ref, sel_ref, o_ref, cext_ref, m_ref)


def _forget_selector():
    sel = np.zeros((N_DIR, 3 * GATE_PAD, W_GROUP), np.float32)
    for d in range(N_DIR):
        for hd in range(N_HEADS):
            fc = d * 2 * N_HEADS + N_HEADS + hd
            for part in range(3):
                sel[d, part * GATE_PAD + fc, hd * HEAD_DIM:(hd + 1) * HEAD_DIM] = 1.0
    return jnp.asarray(sel, BF16)


def _mlstm(P, gates, gates_t, bias_row, bias_col, B, seq, ctx_len):
    M = P.shape[0]
    ncl, ncc = seq // CHUNK, ctx_len // CHUNK
    nc = ncl + ncc
    ctx0 = B * ncl
    fwd = lambda b, j: jnp.where(j < ncc, ctx0 + b * ncc + j, b * ncl + j - ncc)
    bwd = lambda b, j: jnp.where(j < ncc, ctx0 + b * ncc + ncc - 1 - j, b * ncl + nc - 1 - j)
    in_specs, args = [], []
    for chunk in (fwd, bwd):
        for off in (OFF_D_Q, OFF_D_K, OFF_D_V):
            for half in range(2):
                in_specs.append(pl.BlockSpec((CHUNK, HALF), lambda b, j, c=chunk, o=off // HALF + half: (c(b, j), o)))
                args.append(P)
        in_specs.append(pl.BlockSpec((CHUNK, GATE_PAD), lambda b, j, c=chunk: (c(b, j), 0)))
        in_specs.append(pl.BlockSpec((GATE_PAD, CHUNK), lambda b, j, c=chunk: (0, c(b, j))))
        args += [gates, gates_t]
    in_specs += [pl.BlockSpec((1, GATE_PAD), lambda b, j: (0, 0)), pl.BlockSpec((GATE_PAD, 1), lambda b, j: (0, 0)),
                 pl.BlockSpec((N_DIR, 3 * GATE_PAD, W_GROUP), lambda b, j: (0, 0, 0))]
    args += [bias_row, bias_col, _forget_selector()]
    return pl.pallas_call(
        _mlstm_kernel,
        grid=(B, nc),
        in_specs=in_specs,
        out_specs=[pl.BlockSpec((CHUNK, W_GROUP), lambda b, j: (fwd(b, j), 0)),
                   pl.BlockSpec((CHUNK, W_GROUP), lambda b, j: (bwd(b, j), 0))],
        out_shape=[jax.ShapeDtypeStruct((M, W_GROUP), F32), jax.ShapeDtypeStruct((M, W_GROUP), F32)],
        scratch_shapes=[pltpu.VMEM((N_DIR * N_HEADS, 2 * HEAD_DIM, HEAD_DIM), F32),
                        pltpu.VMEM((N_DIR * N_HEADS, 1, HEAD_DIM), F32)],
        compiler_params=_cparams(("parallel", "arbitrary")),
        name="mlstm",
    )(*args)


def _memout_kernel(hf_ref, hb_ref, o_ref, z_ref, g_ref, out_ref):
    for hd in range(HALF // HEAD_DIM):
        cs = slice(hd * HEAD_DIM, (hd + 1) * HEAD_DIM)
        h = hf_ref[:, cs] + hb_ref[:, cs]
        hn = h * lax.rsqrt(jnp.mean(h * h, axis=-1, keepdims=True) + EPS) * g_ref[:, cs]
        out_ref[:, cs] = (hn * jax.nn.sigmoid(o_ref[:, cs].astype(F32)) * _silu(z_ref[:, cs].astype(F32))).astype(BF16)


def _mem_out(hf, hb, P, n_rows, g, rows=512):
    blk = lambda off: pl.BlockSpec((rows, HALF), lambda i, hh: (i, off // HALF + hh))
    return pl.pallas_call(
        _memout_kernel,
        grid=(n_rows // rows, 2),
        in_specs=[blk(0), blk(0), blk(OFF_D_O), blk(OFF_D_Z), pl.BlockSpec((1, HALF), lambda i, hh: (0, hh))],
        out_specs=blk(0),
        out_shape=jax.ShapeDtypeStruct((n_rows, W_GROUP), BF16),
        compiler_params=_cparams(("parallel", "parallel")),
        name="mem_out",
    )(hf, hb, P, P, g.reshape(1, W_GROUP))


def _out_kernel(a_ref, b_ref, c_ref, d_ref, w_ref, res_ref, gate_ref, o_ref):
    x = jnp.concatenate([a_ref[...], b_ref[...], c_ref[...], d_ref[...]], axis=1)
    o_ref[...] = res_ref[...] + gate_ref[...] * jnp.dot(x, w_ref[...], preferred_element_type=F32)


def _out_proj(a, b, att, mem, w_all, layer, res, mod, *, row_block0, gate_row, tm, tn):
    n_rows, D = res.shape
    gcol = 2 * D // tn
    shifted = pl.BlockSpec((tm, W_GROUP), lambda n, m: (row_block0 + m, 0))
    return pl.pallas_call(
        _out_kernel,
        grid=(D // tn, n_rows // tm),
        in_specs=[shifted, shifted, pl.BlockSpec((tm, W_GROUP), lambda n, m: (m, 0)), shifted,
                  pl.BlockSpec((None, 4 * W_GROUP, tn), lambda n, m: (layer, 0, n)),
                  pl.BlockSpec((tm, tn), lambda n, m: (m, n)),
                  pl.BlockSpec((None, 1, tn), lambda n, m: (gate_row(m), 0, gcol + n))],
        out_specs=pl.BlockSpec((tm, tn), lambda n, m: (m, n)),
        out_shape=jax.ShapeDtypeStruct((n_rows, D), F32),
        compiler_params=_cparams(("parallel", "parallel"), V7X_VMEM_LIMIT),
        name="out_proj",
    )(a, b, att, mem, w_all, res, mod)


def kernel(x, c, ctx, c_ctx, w_ada, b_ada, norm_g, w_in, sgu_w, sgu_b, sgu_ln_g, sgu_ln_b, conv_w, conv_b,
           conv_ln_g, conv_ln_b, q_norm_g, k_norm_g, mlstm_i_bias, mlstm_f_bias, mh_norm_g, w_out):
    B, seq, D = x.shape
    ctx_len = ctx.shape[1]
    depth = w_ada.shape[0]
    ML, MC = B * seq, B * ctx_len
    M = ML + MC
    assert ctx_len == ROW_TILE and seq % 512 == 0 and M % 512 == 0 and D == 4 * W_GROUP and B < 8
    assert w_in.shape[2] == P_MAIN + GATE_COLS

    h_lat, h_ctx = x.reshape(ML, D), ctx.reshape(MC, D)
    cc = jnp.zeros((8, D), F32).at[:B].set(c).at[B].set(c_ctx)
    mod_all = _ada_mod(cc, w_ada, b_ada)
    cos, sin = _rope_tables(seq)
    w_out_b, sgu_w_b = w_out.astype(BF16), sgu_w.astype(BF16)
    w_in_t = jnp.swapaxes(w_in, 1, 2)
    tm_in = M // 8

    for l in range(depth):
        last = l == depth - 1
        n_rows = ML if last else M
        wgt = jnp.pad(w_in_t[l, P_MAIN:, :], ((0, GATE_PAD - GATE_COLS), (0, 0)))
        mod = mod_all[l].reshape(8, 1, 3 * D)
        hn, gates, gates_t = _norm_mod(h_lat, h_ctx, mod, norm_g[l], wgt.T, wgt, B, seq)
        P = _in_proj(hn, w_in_t, l, tm=tm_in, tn=768)

        a_out = _gmlp(P, n_rows, sgu_w_b[l], sgu_b[l].reshape(N_HEADS, CHUNK, 1), sgu_ln_g[l], sgu_ln_b[l])
        b_out = _conv(P, n_rows, ML, seq, conv_w[l], conv_b[l], conv_ln_g[l], conv_ln_b[l])

        qr, kr = _qk_prep(P, cos, sin, q_norm_g[l], k_norm_g[l], ML, seq)
        att_lat = _attend(qr, kr, P, B, tq=1024, rows=512, kb=seq, q_block0=0, n_q=seq,
                          key_sets=((seq, 0), (ctx_len, ML // ctx_len)))

        bias = jnp.concatenate([mlstm_i_bias[l], mlstm_f_bias[l]], axis=-1).reshape(GATE_COLS)
        bias = jnp.pad(bias, (0, GATE_PAD - GATE_COLS))
        hf, hb = _mlstm(P, gates, gates_t, bias.reshape(1, GATE_PAD), bias.reshape(GATE_PAD, 1), B, seq, ctx_len)
        mem = _mem_out(hf, hb, P, n_rows, mh_norm_g[l])

        if not last:
            att_ctx = _attend(qr, kr, P, B, tq=ctx_len, rows=128, kb=ctx_len, q_block0=ML // ctx_len, n_q=ctx_len,
                              key_sets=((ctx_len, ML // ctx_len),))
            h_ctx = _out_proj(a_out, b_out, att_ctx, mem, w_out_b, l, h_ctx, mod, row_block0=ML // MC,
                              gate_row=lambda m: B, tm=MC, tn=1024)
        h_lat = _out_proj(a_out, b_out, att_lat, mem, w_out_b, l, h_lat, mod, row_block0=0,
                          gate_row=lambda m: (m * 512) // seq, tm=512, tn=1024)

    return h_lat.reshape(B, seq, D)
```
